```python
import jax, jax.numpy as jnp
from jax import lax
import numpy as np

D_MODEL = 2048
BATCH = 8
SEQ = 2048
DEPTH = 1

CHUNK = 64
NORM_EPS = 1e-6
N_ADA = 6

A_HEADS = 8
A_HEAD_DIM = 128
A_WIDTH = A_HEADS * A_HEAD_DIM
IDX_HEADS = 16
IDX_DIM = 64
TOPK_MAX = 256
Q_BLOCK = 64

B_HEADS = 8
B_HEAD_DIM = 128
B_WIDTH = B_HEADS * B_HEAD_DIM

PEER_HEADS = 8
PEER_N_KEYS = 128
PEER_N_EXPERTS = PEER_N_KEYS * PEER_N_KEYS
PEER_KEY_DIM = 256
PEER_HALF = PEER_KEY_DIM // 2
PEER_TOPK = 16
PEER_TOKEN_BLOCK = 128

IN_WIDTHS = (A_WIDTH, A_WIDTH, A_WIDTH, IDX_HEADS * IDX_DIM, IDX_DIM, IDX_HEADS,
             B_WIDTH, B_WIDTH, B_WIDTH, B_WIDTH, D_MODEL, D_MODEL)
IN_WIDTH = 3 * A_WIDTH + IDX_HEADS * IDX_DIM + IDX_DIM + IDX_HEADS + 4 * B_WIDTH + 2 * D_MODEL

kernel_name = "hybrid_dsa_hgrn2_peer_adaln_block"


def rms_norm(x, gain):
    xf = x.astype(jnp.float32)
    y = xf * lax.rsqrt(jnp.mean(xf * xf, axis=-1, keepdims=True) + NORM_EPS)
    return (y * gain.astype(jnp.float32)).astype(x.dtype)


def modulate(h, shift, scale):
    return h * (1 + scale[:, None, :]) + shift[:, None, :]


def split_columns(a, widths):
    parts, start = [], 0
    for w in widths:
        parts.append(a[..., start:start + w])
        start += w
    return parts


def dsa_attention(q, k, v, q_idx, k_idx, w_idx):
    B, S = q.shape[0], q.shape[1]
    top_k = min(TOPK_MAX, S // 4)
    n_blk = S // Q_BLOCK
    key_chunk = jnp.arange(S) // CHUNK
    idx_scale = (IDX_DIM * IDX_HEADS) ** -0.5
    att_scale = A_HEAD_DIM ** -0.5
    gather = jax.vmap(lambda table, ids: table[ids])

    def to_blocks(a):
        return jnp.moveaxis(a.reshape((B, n_blk, Q_BLOCK) + a.shape[2:]), 1, 0)

    def block(args):
        qb, qib, wb, blk = args
        q_chunk = (blk * Q_BLOCK + jnp.arange(Q_BLOCK)) // CHUNK
        admissible = key_chunk[None, :] <= q_chunk[:, None]
        dots = jnp.einsum('bqhd,bsd->bqhs', qib, k_idx)
        score = jnp.einsum('bqh,bqhs->bqs', wb, jax.nn.relu(dots)).astype(jnp.float32) * idx_scale
        score = jnp.where(admissible[None], score, -jnp.inf)
        _, sel = lax.top_k(score, top_k)
        valid = jnp.take(key_chunk, sel) <= q_chunk[None, :, None]
        k_sel = gather(k, sel)
        v_sel = gather(v, sel)
        logits = jnp.einsum('bqhd,bqkhd->bqhk', qb, k_sel).astype(jnp.float32) * att_scale
        logits = jnp.where(valid[:, :, None, :], logits, -jnp.inf)
        p = jax.nn.softmax(logits, axis=-1).astype(v.dtype)
        return jnp.einsum('bqhk,bqkhd->bqhd', p, v_sel)

    out = lax.map(block, (to_blocks(q), to_blocks(q_idx), to_blocks(w_idx), jnp.arange(n_blk)))
    return jnp.moveaxis(out, 0, 1).reshape(B, S, A_HEADS, A_HEAD_DIM)


def hgrn2(q, f_logit, i, lower_bound):
    B, S, H, D = q.shape
    out_dtype = q.dtype
    n_c = S // CHUNK
    lb = lower_bound.reshape(H, D).astype(jnp.float32)
    f = lb + (1 - lb) * jax.nn.sigmoid(f_logit.astype(jnp.float32))
    log_f = jnp.log(f)
    key = 1 - f
    qf = jax.nn.silu(q.astype(jnp.float32))
    val = i.astype(jnp.float32)
    causal = jnp.tril(jnp.ones((CHUNK, CHUNK), dtype=bool))

    def chunks(a):
        return a.reshape(B, n_c, CHUNK, H, D).transpose(1, 0, 3, 2, 4)

    def step(state, xs):
        qc, kc, vc, lfc = xs
        b = jnp.cumsum(lfc, axis=2)
        b_last = b[:, :, -1:, :]
        o_inter = jnp.einsum('bhtd,bhde->bhte', qc * jnp.exp(b), state)
        diff = b[:, :, :, None, :] - b[:, :, None, :, :]
        decay = jnp.exp(jnp.where(causal[None, None, :, :, None], diff, -jnp.inf))
        attn = jnp.einsum('bhtd,bhsd,bhtsd->bhts', qc, kc, decay)
        o_intra = jnp.einsum('bhts,bhse->bhte', attn, vc)
        k_to_end = kc * jnp.exp(b_last - b)
        new_state = (jnp.exp(b_last)[:, :, 0, :, None] * state
                     + jnp.einsum('bhsd,bhse->bhde', k_to_end, vc))
        return new_state, o_inter + o_intra

    state0 = jnp.zeros((B, H, D, D), jnp.float32)
    _, o = lax.scan(step, state0, (chunks(qf), chunks(key), chunks(val), chunks(log_f)))
    return o.transpose(1, 0, 3, 2, 4).reshape(B, S, H, D).astype(out_dtype)


def peer(h, w_q, sub_keys, u, v):
    B, S, D = h.shape
    T = B * S
    hf = h.reshape(T, D)
    q = (hf @ w_q).reshape(T, PEER_HEADS, 2, PEER_HALF)
    s = jnp.einsum('tphd,phnd->tphn', q, sub_keys).astype(jnp.float32)
    top_s, top_i = lax.top_k(s, PEER_TOPK)
    cand_s = top_s[:, :, 0, :, None] + top_s[:, :, 1, None, :]
    cand_i = top_i[:, :, 0, :, None] * PEER_N_KEYS + top_i[:, :, 1, None, :]
    best_s, best_pos = lax.top_k(cand_s.reshape(T, PEER_HEADS, PEER_TOPK * PEER_TOPK), PEER_TOPK)
    expert = jnp.take_along_axis(cand_i.reshape(T, PEER_HEADS, PEER_TOPK * PEER_TOPK), best_pos, axis=-1)
    gate = jax.nn.softmax(best_s, axis=-1)
    n_blk = T // PEER_TOKEN_BLOCK
    n_sel = PEER_HEADS * PEER_TOPK

    def block(args):
        hb, eb, gb = args
        ub = u[eb]
        vb = v[eb]
        act = jax.nn.gelu(jnp.einsum('td,ted->te', hb, ub).astype(jnp.float32), approximate=False) * gb
        return jnp.einsum('te,ted->td', act.astype(vb.dtype), vb)

    out = lax.map(block, (hf.reshape(n_blk, PEER_TOKEN_BLOCK, D),
                          expert.reshape(n_blk, PEER_TOKEN_BLOCK, n_sel),
                          gate.reshape(n_blk, PEER_TOKEN_BLOCK, n_sel)))
    return out.reshape(B, S, D).astype(h.dtype)


def setup_inputs(seed: int = 0) -> dict:
    key = jax.random.key(seed)
    ks = jax.random.split(key, 20)
    f32 = jnp.float32
    nrm = lambda k, shape, s: jax.random.normal(k, shape, f32) * s
    return {
        "x": nrm(ks[0], (BATCH, SEQ, D_MODEL), 1.0),
        "c": nrm(ks[1], (BATCH, D_MODEL), 1.0),
        "w_ada": nrm(ks[2], (DEPTH, D_MODEL, N_ADA * D_MODEL), 0.5 * D_MODEL ** -0.5),
        "b_ada": nrm(ks[3], (DEPTH, N_ADA * D_MODEL), 0.01),
        "norm_mix": 1.0 + nrm(ks[4], (DEPTH, D_MODEL), 0.01),
        "norm_ffn": 1.0 + nrm(ks[5], (DEPTH, D_MODEL), 0.01),
        "w_in": nrm(ks[6], (DEPTH, D_MODEL, IN_WIDTH), D_MODEL ** -0.5),
        "lb_logits": nrm(ks[7], (DEPTH + 1, B_WIDTH), 0.5),
        "hgrn_gain": 1.0 + nrm(ks[8], (DEPTH, B_HEAD_DIM), 0.01),
        "w_up_a": nrm(ks[9], (DEPTH, A_WIDTH, D_MODEL), A_WIDTH ** -0.5),
        "w_up_b": nrm(ks[10], (DEPTH, B_WIDTH, D_MODEL), B_WIDTH ** -0.5),
        "w_out": nrm(ks[11], (DEPTH, D_MODEL, D_MODEL), D_MODEL ** -0.5),
        "peer_w_q": nrm(ks[12], (DEPTH, D_MODEL, PEER_HEADS * PEER_KEY_DIM), D_MODEL ** -0.5),
        "peer_keys": nrm(ks[13], (DEPTH, PEER_HEADS, 2, PEER_N_KEYS, PEER_HALF), PEER_HALF ** -0.5),
        "peer_u": nrm(ks[14], (DEPTH, PEER_N_EXPERTS, D_MODEL), D_MODEL ** -0.5),
        "peer_v": nrm(ks[15], (DEPTH, PEER_N_EXPERTS, D_MODEL), PEER_HEADS ** -0.5),
        "final_norm": 1.0 + nrm(ks[16], (D_MODEL,), 0.01),
    }


def reference(x, c, w_ada, b_ada, norm_mix, norm_ffn, w_in, lb_logits, hgrn_gain,
              w_up_a, w_up_b, w_out, peer_w_q, peer_keys, peer_u, peer_v, final_norm):
    B, S, _ = x.shape
    lower_bounds = jnp.cumsum(jax.nn.softmax(lb_logits.astype(jnp.float32), axis=0), axis=0)
    cond = jax.nn.silu(c)
    for l in range(DEPTH):
        mod = cond @ w_ada[l] + b_ada[l]
        shift1, scale1, gate1, shift2, scale2, gate2 = jnp.split(mod, N_ADA, axis=-1)

        h = modulate(rms_norm(x, norm_mix[l]), shift1, scale1)
        proj = h @ w_in[l]
        qa, ka, va, qi, ki, wi, qb, fb, ib, gb, g_a, g_b = split_columns(proj, IN_WIDTHS)
        attn = dsa_attention(qa.reshape(B, S, A_HEADS, A_HEAD_DIM),
                             ka.reshape(B, S, A_HEADS, A_HEAD_DIM),
                             va.reshape(B, S, A_HEADS, A_HEAD_DIM),
                             qi.reshape(B, S, IDX_HEADS, IDX_DIM), ki, wi)
        rec = hgrn2(qb.reshape(B, S, B_HEADS, B_HEAD_DIM),
                    fb.reshape(B, S, B_HEADS, B_HEAD_DIM),
                    ib.reshape(B, S, B_HEADS, B_HEAD_DIM), lower_bounds[l])
        rec = rms_norm(rec, hgrn_gain[l]) * jax.nn.silu(gb.reshape(B, S, B_HEADS, B_HEAD_DIM))
        y_a = attn.reshape(B, S, A_WIDTH) @ w_up_a[l]
        y_b = rec.reshape(B, S, B_WIDTH) @ w_up_b[l]
        mixed = jax.nn.sigmoid(g_a) * y_a + jax.nn.sigmoid(g_b) * y_b
        x = x + gate1[:, None, :] * (mixed @ w_out[l])

        h = modulate(rms_norm(x, norm_ffn[l]), shift2, scale2)
        x = x + gate2[:, None, :] * peer(h, peer_w_q[l], peer_keys[l], peer_u[l], peer_v[l])
    return rms_norm(x, final_norm)
```

```python
import functools

import jax
import jax.numpy as jnp
from jax import lax
from jax.experimental import pallas as pl
from jax.experimental.pallas import tpu as pltpu

F32 = jnp.float32
BF16 = jnp.bfloat16

NORM_EPS = 1e-6
CHUNK = 64
TOPK_MAX = 256
A_HEADS = 8
HEAD_DIM = 128
IDX_HEADS = 16
IDX_DIM = 64
B_HEADS = 8
PEER_HEADS = 8
PEER_N_KEYS = 128
PEER_TOPK = 16
SUB = 16

VMEM_LIMIT = 56 * 1024 * 1024
INT_MIN = -2 ** 31

NT_DIMS = (((1,), (1,)), ((), ()))
TN_DIMS = (((0,), (0,)), ((), ()))


def _sigmoid(x):
    return 1.0 / (1.0 + jnp.exp(-x))


def _params(sem):
    return pltpu.CompilerParams(dimension_semantics=sem, vmem_limit_bytes=VMEM_LIMIT)


def _ada_kernel(c_ref, w_ref, b_ref, o_ref):
    c = c_ref[...]
    cond = (c * _sigmoid(c)).astype(BF16)
    o_ref[...] = jnp.dot(cond, w_ref[...].astype(BF16), preferred_element_type=F32) + b_ref[...]


def _ada(c, w, b, tn=1024):
    bsz, d = c.shape
    n = w.shape[1]
    return pl.pallas_call(
        _ada_kernel,
        out_shape=jax.ShapeDtypeStruct((bsz, n), F32),
        grid=(n // tn,),
        in_specs=[pl.BlockSpec((bsz, d), lambda j: (0, 0)),
                  pl.BlockSpec((d, tn), lambda j: (0, j)),
                  pl.BlockSpec((1, tn), lambda j: (0, j))],
        out_specs=pl.BlockSpec((bsz, tn), lambda j: (0, j)),
        compiler_params=_params(("arbitrary",)),
        name="ada",
    )(c, w, b.reshape(1, n))


def _norm_mod(x, gain, scale, shift):
    ms = jnp.mean(x * x, axis=-1, keepdims=True)
    y = x * lax.rsqrt(ms + NORM_EPS) * gain
    return y * (1.0 + scale) + shift


def _inproj_kernel(x_ref, g_ref, sc_ref, sh_ref, w_ref, o_ref, h_ref):
    @pl.when(pl.program_id(1) == 0)
    def _():
        h_ref[...] = _norm_mod(x_ref[...], g_ref[...], sc_ref[0], sh_ref[0]).astype(BF16)

    o_ref[...] = jnp.dot(h_ref[...], w_ref[...], preferred_element_type=F32).astype(o_ref.dtype)


def _inproj(x2d, gain, scale, shift, w, out_dtype, seq, tm, tn):
    t, d = x2d.shape
    n = w.shape[1]
    per_b = seq // tm
    return pl.pallas_call(
        _inproj_kernel,
        out_shape=jax.ShapeDtypeStruct((t, n), out_dtype),
        grid=(t // tm, n // tn),
        in_specs=[pl.BlockSpec((tm, d), lambda i, j: (i, 0)),
                  pl.BlockSpec((1, d), lambda i, j: (0, 0)),
                  pl.BlockSpec((1, 1, d), lambda i, j: (i // per_b, 0, 0)),
                  pl.BlockSpec((1, 1, d), lambda i, j: (i // per_b, 0, 0)),
                  pl.BlockSpec((d, tn), lambda i, j: (0, j))],
        out_specs=pl.BlockSpec((tm, tn), lambda i, j: (i, j)),
        scratch_shapes=[pltpu.VMEM((tm, d), BF16)],
        compiler_params=_params(("arbitrary", "arbitrary")),
        name="inproj",
    )(x2d, gain.reshape(1, d), scale, shift, w)


def _dsa_kernel(qi_ref, kw_ref, qa_ref, ka_ref, va_ref, o_ref, key_ref, bias_ref):
    blk = pl.program_id(1)
    qn = qi_ref.shape[0]
    s = ka_ref.shape[0]
    r0 = pl.multiple_of(blk * qn, qn)

    ki = kw_ref[:, 0:IDX_DIM].astype(BF16)
    wq = kw_ref[pl.ds(r0, qn), IDX_DIM:IDX_DIM + IDX_HEADS].astype(F32)
    score = jnp.zeros((qn, s), F32)
    for h in range(IDX_HEADS):
        qh = qi_ref[:, h * IDX_DIM:(h + 1) * IDX_DIM]
        dots = lax.dot_general(qh, ki, NT_DIMS, preferred_element_type=F32)
        score = score + wq[:, h:h + 1] * jnp.maximum(dots, 0.0)
    score = score * ((IDX_DIM * IDX_HEADS) ** -0.5)

    bits = pltpu.bitcast(score, jnp.int32)
    skey = bits ^ ((bits >> 31) & jnp.int32(0x7FFFFFFF))
    key_chunk = lax.broadcasted_iota(jnp.int32, (qn, s), 1) // CHUNK
    q_chunk = (r0 + lax.broadcasted_iota(jnp.int32, (qn, s), 0)) // CHUNK
    key_ref[...] = jnp.where(key_chunk <= q_chunk, skey, jnp.int32(INT_MIN))

    top_k = float(min(TOPK_MAX, s // 4))

    def bit_step(it, v):
        cand = v + lax.shift_left(jnp.int32(1), 31 - it)
        cnt = jnp.sum(jnp.where(key_ref[...] >= cand, 1.0, 0.0), axis=1, keepdims=True)
        return jnp.where(cnt >= top_k, cand, v)

    v = lax.fori_loop(0, 32, bit_step, jnp.full((qn, 1), INT_MIN, jnp.int32))
    v = jnp.maximum(v, jnp.int32(INT_MIN + 1))
    bias_ref[...] = jnp.where(key_ref[...] >= v, 0.0, -jnp.inf)

    att_scale = HEAD_DIM ** -0.5
    for h in range(A_HEADS):
        cols = slice(h * HEAD_DIM, (h + 1) * HEAD_DIM)
        logits = lax.dot_general(qa_ref[:, cols], ka_ref[:, cols], NT_DIMS,
                                 preferred_element_type=F32) * att_scale + bias_ref[...]
        m = jnp.max(logits, axis=1, keepdims=True)
        p = jnp.exp(logits - m)
        l = jnp.sum(p, axis=1, keepdims=True)
        o = jnp.dot(p.astype(BF16), va_ref[:, cols], preferred_element_type=F32)
        o_ref[:, cols] = (o / l).astype(o_ref.dtype)


def _dsa(p1, p2, bsz, seq, col):
    n_blk = seq // CHUNK
    width = A_HEADS * HEAD_DIM
    return pl.pallas_call(
        _dsa_kernel,
        out_shape=jax.ShapeDtypeStruct((bsz * seq, width), BF16),
        grid=(bsz, n_blk),
        in_specs=[pl.BlockSpec((CHUNK, width), lambda b, k: (b * n_blk + k, col["qi"])),
                  pl.BlockSpec((seq, 128), lambda b, k: (b, col["kw"])),
                  pl.BlockSpec((CHUNK, width), lambda b, k: (b * n_blk + k, col["qa"])),
                  pl.BlockSpec((seq, width), lambda b, k: (b, col["ka"])),
                  pl.BlockSpec((seq, width), lambda b, k: (b, col["va"]))],
        out_specs=pl.BlockSpec((CHUNK, width), lambda b, k: (b * n_blk + k, 0)),
        scratch_shapes=[pltpu.VMEM((CHUNK, seq), jnp.int32), pltpu.VMEM((CHUNK, seq), F32)],
        compiler_params=_params(("arbitrary", "arbitrary")),
        name="dsa",
    )(p1, p2, p1, p1, p1)


def _hgrn_kernel(q_ref, f_ref, i_ref, g_ref, lb_ref, gain_ref, o_ref):
    c = CHUNK
    n_c = q_ref.shape[0] // c
    d = q_ref.shape[1]
    lb = lb_ref[0]
    gain = gain_ref[...]
    row = lax.broadcasted_iota(jnp.int32, (c, c), 0)
    colm = lax.broadcasted_iota(jnp.int32, (c, c), 1)
    tri = jnp.where(colm <= row, 1.0, 0.0).astype(F32)
    sub_row = lax.broadcasted_iota(jnp.int32, (SUB, d), 0)

    def chunk(ci, st_t):
        r0 = pl.multiple_of(ci * c, c)
        f = lb + (1.0 - lb) * _sigmoid(f_ref[pl.ds(r0, c), :])
        logf = jnp.log(f)
        k = 1.0 - f
        qr = q_ref[pl.ds(r0, c), :].astype(F32)
        q = qr * _sigmoid(qr)
        v = i_ref[pl.ds(r0, c), :].astype(F32)
        v16 = v.astype(BF16)
        b = jnp.dot(tri, logf, preferred_element_type=F32, precision=lax.Precision.HIGHEST)
        b_last = b[c - 1:c, :]

        o_inter = lax.dot_general((q * jnp.exp(b)).astype(BF16), st_t.astype(BF16), NT_DIMS,
                                  preferred_element_type=F32)
        outs = []
        for blk in range(c // SUB):
            lo = blk * SUB
            q_i, k_i, v_i, b_i = q[lo:lo + SUB], k[lo:lo + SUB], v[lo:lo + SUB], b[lo:lo + SUB]
            o_i = o_inter[lo:lo + SUB]
            if blk > 0:
                b_ref = b[lo - 1:lo, :]
                qt = (q_i * jnp.exp(b_i - b_ref)).astype(BF16)
                kt = (k[:lo] * jnp.exp(b_ref - b[:lo])).astype(BF16)
                att = lax.dot_general(qt, kt, NT_DIMS, preferred_element_type=F32)
                o_i = o_i + jnp.dot(att.astype(BF16), v16[:lo], preferred_element_type=F32)
            for s_ in range(SUB):
                w = jnp.exp(jnp.where(sub_row >= s_, b_i - b_i[s_:s_ + 1, :], -jnp.inf))
                a = jnp.sum(q_i * k_i[s_:s_ + 1, :] * w, axis=1, keepdims=True)
                o_i = o_i + a * v_i[s_:s_ + 1, :]
            outs.append(o_i)
        o = jnp.concatenate(outs, axis=0)

        ms = jnp.mean(o * o, axis=-1, keepdims=True)
        rec = o * lax.rsqrt(ms + NORM_EPS) * gain
        gr = g_ref[pl.ds(r0, c), :].astype(F32)
        o_ref[pl.ds(r0, c), :] = (rec * (gr * _sigmoid(gr))).astype(o_ref.dtype)

        k_end = (k * jnp.exp(b_last - b)).astype(BF16)
        upd = lax.dot_general(v16, k_end, TN_DIMS, preferred_element_type=F32)
        return st_t * jnp.exp(b_last) + upd

    lax.fori_loop(0, n_c, chunk, jnp.zeros((d, d), F32))


def _hgrn(p1, p2, lb, gain, bsz, seq, col):
    d = HEAD_DIM
    return pl.pallas_call(
        _hgrn_kernel,
        out_shape=jax.ShapeDtypeStruct((bsz * seq, B_HEADS * d), BF16),
        grid=(bsz, B_HEADS),
        in_specs=[pl.BlockSpec((seq, d), lambda b, h: (b, col["qb"] + h)),
                  pl.BlockSpec((seq, d), lambda b, h: (b, col["fb"] + h)),
                  pl.BlockSpec((seq, d), lambda b, h: (b, col["ib"] + h)),
                  pl.BlockSpec((seq, d), lambda b, h: (b, col["gb"] + h)),
                  pl.BlockSpec((1, 1, d), lambda b, h: (h, 0, 0)),
                  pl.BlockSpec((1, d), lambda b, h: (0, 0))],
        out_specs=pl.BlockSpec((seq, d), lambda b, h: (b, h)),
        compiler_params=_params(("arbitrary", "arbitrary")),
        name="hgrn",
    )(p1, p2, p1, p1, lb.reshape(B_HEADS, 1, d), gain.reshape(1, d))


def _mix_kernel(attn_ref, rec_ref, ga_ref, gb_ref, x_ref, gate_ref, wa_ref, wb_ref, wo_ref,
                g2_ref, sc_ref, sh_ref, wq_ref, x1_ref, h2_ref, qp_ref):
    ya = jnp.dot(attn_ref[...], wa_ref[...], preferred_element_type=F32)
    yb = jnp.dot(rec_ref[...], wb_ref[...], preferred_element_type=F32)
    mixed = _sigmoid(ga_ref[...].astype(F32)) * ya + _sigmoid(gb_ref[...].astype(F32)) * yb
    x1 = x_ref[...] + gate_ref[0] * jnp.dot(mixed.astype(BF16), wo_ref[...],
                                            preferred_element_type=F32)
    x1_ref[...] = x1
    h2 = _norm_mod(x1, g2_ref[...], sc_ref[0], sh_ref[0]).astype(BF16)
    h2_ref[...] = h2
    qp_ref[...] = jnp.dot(h2, wq_ref[...], preferred_element_type=F32).astype(qp_ref.dtype)


def _mix(attn, rec, p1, x2d, gate1, wa, wb, wo, g2, scale2, shift2, wq, seq, tm, col):
    t, d = x2d.shape
    wa_w = attn.shape[1]
    per_b = seq // tm
    full = lambda shape: pl.BlockSpec(shape, lambda i: (0,) * len(shape),
                                      pipeline_mode=pl.Buffered(1))
    mod = pl.BlockSpec((1, 1, d), lambda i: (i // per_b, 0, 0))
    return pl.pallas_call(
        _mix_kernel,
        out_shape=(jax.ShapeDtypeStruct((t, d), F32), jax.ShapeDtypeStruct((t, d), BF16),
                   jax.ShapeDtypeStruct((t, wq.shape[1]), BF16)),
        grid=(t // tm,),
        in_specs=[pl.BlockSpec((tm, wa_w), lambda i: (i, 0)),
                  pl.BlockSpec((tm, wa_w), lambda i: (i, 0)),
                  pl.BlockSpec((tm, d), lambda i: (i, col["g_a"])),
                  pl.BlockSpec((tm, d), lambda i: (i, col["g_b"])),
                  pl.BlockSpec((tm, d), lambda i: (i, 0)),
                  mod, full(wa.shape), full(wb.shape), full(wo.shape),
                  full((1, d)), mod, mod, full(wq.shape)],
        out_specs=(pl.BlockSpec((tm, d), lambda i: (i, 0)),
                   pl.BlockSpec((tm, d), lambda i: (i, 0)),
                   pl.BlockSpec((tm, wq.shape[1]), lambda i: (i, 0))),
        compiler_params=_params(("arbitrary",)),
        name="mix",
    )(attn, rec, p1, p1, x2d, gate1, wa, wb, wo, g2.reshape(1, d), scale2, shift2, wq)


def _top_values(work, n, dst_ref):
    for r in range(n):
        mx = jnp.max(work, axis=0, keepdims=True)
        dst_ref[r:r + 1, :] = mx
        work = jnp.where(work == mx, -jnp.inf, work)


def _gelu(x):
    return 0.5 * x * (1.0 + lax.erf(x * (2.0 ** -0.5)))


def _peer_kernel(qp_ref, keys_ref, h2_ref, u_ref, vt_ref, x1_ref, gate_ref, fin_ref, o_ref,
                 s0_ref, s1_ref, c0_ref, e1_ref, thr_ref, act_ref, acc_ref,
                 top_ref, cand_ref, best_ref):
    eb = pl.program_id(1)
    n_eb = pl.num_programs(1)
    nk = PEER_N_KEYS
    rows_per_step = u_ref.shape[0] // nk
    tm = h2_ref.shape[0]

    @pl.when(eb == 0)
    def _():
        acc_ref[...] = jnp.zeros_like(acc_ref)
        kk = PEER_TOPK
        for p in range(PEER_HEADS):
            for half, s_ref in ((0, s0_ref), (1, s1_ref)):
                j = 2 * p + half
                qs = qp_ref[:, j * nk:(j + 1) * nk]
                st = lax.dot_general(keys_ref[j], qs, NT_DIMS, preferred_element_type=F32)
                s_ref[p] = st
                _top_values(st, kk, top_ref.at[half])
            for a in range(kk):
                cand_ref[a * kk:(a + 1) * kk, :] = top_ref[0, a:a + 1, :] + top_ref[1]
            _top_values(cand_ref[...], kk, best_ref)
            best = best_ref[...]
            z = jnp.sum(jnp.exp(best - best[0:1, :]), axis=0, keepdims=True)
            thr_ref[p] = best[kk - 1:kk, :]
            c0_ref[p] = jnp.exp(s0_ref[p] - top_ref[0, 0:1, :]) / z
            e1_ref[p] = jnp.exp(s1_ref[p] - top_ref[1, 0:1, :])

    a_t = lax.dot_general(u_ref[...], h2_ref[...], NT_DIMS, preferred_element_type=F32)
    for r in range(rows_per_step):
        i = eb * rows_per_step + r
        g = jnp.zeros((nk, tm), F32)
        for p in range(PEER_HEADS):
            ssum = s0_ref[p, pl.ds(i, 1), :] + s1_ref[p]
            val = c0_ref[p, pl.ds(i, 1), :] * e1_ref[p]
            g = g + jnp.where(ssum >= thr_ref[p], val, 0.0)
        act_ref[r * nk:(r + 1) * nk, :] = (_gelu(a_t[r * nk:(r + 1) * nk, :]) * g).astype(BF16)
    acc_ref[...] += jnp.dot(vt_ref[...], act_ref[...], preferred_element_type=F32)

    @pl.when(eb == n_eb - 1)
    def _():
        x2 = x1_ref[...] + gate_ref[0] * acc_ref[...].T
        ms = jnp.mean(x2 * x2, axis=-1, keepdims=True)
        o_ref[...] = x2 * lax.rsqrt(ms + NORM_EPS) * fin_ref[...]


def _peer(qp, keys, h2, u, vt, x1, gate2, fin, seq, tm, te):
    t, d = x1.shape
    n_exp = u.shape[0]
    per_b = seq // tm
    nk = PEER_N_KEYS
    stat = pltpu.VMEM((PEER_HEADS, nk, tm), F32)
    return pl.pallas_call(
        _peer_kernel,
        out_shape=jax.ShapeDtypeStruct((t, d), F32),
        grid=(t // tm, n_exp // te),
        in_specs=[pl.BlockSpec((tm, qp.shape[1]), lambda i, e: (i, 0)),
                  pl.BlockSpec(keys.shape, lambda i, e: (0, 0, 0)),
                  pl.BlockSpec((tm, d), lambda i, e: (i, 0)),
                  pl.BlockSpec((te, d), lambda i, e: (e, 0)),
                  pl.BlockSpec((d, te), lambda i, e: (0, e)),
                  pl.BlockSpec((tm, d), lambda i, e: (i, 0)),
                  pl.BlockSpec((1, 1, d), lambda i, e: (i // per_b, 0, 0)),
                  pl.BlockSpec((1, d), lambda i, e: (0, 0))],
        out_specs=pl.BlockSpec((tm, d), lambda i, e: (i, 0)),
        scratch_shapes=[stat, stat, stat, stat,
                        pltpu.VMEM((PEER_HEADS, 1, tm), F32),
                        pltpu.VMEM((te, tm), BF16),
                        pltpu.VMEM((d, tm), F32),
                        pltpu.VMEM((2, PEER_TOPK, tm), F32),
                        pltpu.VMEM((PEER_TOPK * PEER_TOPK, tm), F32),
                        pltpu.VMEM((PEER_TOPK, tm), F32)],
        compiler_params=_params(("arbitrary", "arbitrary")),
        name="peer",
    )(qp, keys, h2, u, vt, x1, gate2, fin.reshape(1, d))


def kernel(x, c, w_ada, b_ada, norm_mix, norm_ffn, w_in, lb_logits, hgrn_gain, w_up_a, w_up_b,
           w_out, peer_w_q, peer_keys, peer_u, peer_v, final_norm):
    bsz, seq, d = x.shape
    depth = w_ada.shape[0]
    assert depth == 1, "the PEER kernel applies the final norm, so it must be the last layer"
    aw = A_HEADS * HEAD_DIM
    bw = B_HEADS * HEAD_DIM
    iw = IDX_HEADS * IDX_DIM
    lower_bounds = jnp.cumsum(jax.nn.softmax(lb_logits.astype(F32), axis=0), axis=0)
    xt = x.reshape(bsz * seq, d)

    widths = (aw, aw, aw, iw, IDX_DIM, IDX_HEADS, bw, bw, bw, bw, d, d)
    names = ("qa", "ka", "va", "qi", "ki", "wi", "qb", "fb", "ib", "gb", "g_a", "g_b")
    off, start = {}, 0
    for nme, wd in zip(names, widths):
        off[nme] = (start, start + wd)
        start += wd
    order1 = ("g_a", "g_b", "qa", "ka", "va", "qi", "qb", "ib", "gb")
    order2 = ("fb", "ki", "wi")
    pad2 = 128 - IDX_DIM - IDX_HEADS

    for l in range(depth):
        mod = _ada(c, w_ada[l], b_ada[l])
        shift1, scale1, gate1, shift2, scale2, gate2 = [
            m.reshape(bsz, 1, d) for m in jnp.split(mod, 6, axis=-1)]

        wl = w_in[l]
        w1 = jnp.concatenate([wl[:, off[n][0]:off[n][1]] for n in order1], axis=1).astype(BF16)
        w2 = jnp.concatenate([wl[:, off[n][0]:off[n][1]] for n in order2]
                             + [jnp.zeros((d, pad2), wl.dtype)], axis=1).astype(BF16)
        p1 = _inproj(xt, norm_mix[l], scale1, shift1, w1, BF16, seq, tm=1024, tn=1024)
        p2 = _inproj(xt, norm_mix[l], scale1, shift1, w2, F32, seq, tm=1024, tn=w2.shape[1])

        attn = _dsa(p1, p2, bsz, seq, {"qa": 4, "ka": 5, "va": 6, "qi": 7, "kw": 8})
        rec = _hgrn(p1, p2, lower_bounds[l], hgrn_gain[l], bsz, seq,
                    {"qb": 64, "ib": 72, "gb": 80, "fb": 0})

        x1, h2, qp = _mix(attn, rec, p1, xt, gate1,
                          w_up_a[l].astype(BF16), w_up_b[l].astype(BF16), w_out[l].astype(BF16),
                          norm_ffn[l], scale2, shift2, peer_w_q[l].astype(BF16),
                          seq, tm=256, col={"g_a": 0, "g_b": 1})
        keys = peer_keys[l].reshape(2 * PEER_HEADS, PEER_N_KEYS, -1).astype(BF16)
        xt = _peer(qp, keys, h2, peer_u[l].astype(BF16), peer_v[l].T.astype(BF16), x1, gate2,
                   final_norm, seq, tm=512, te=512)
    return xt.reshape(bsz, seq, d)
```

```python
import functools

import jax
import jax.numpy as jnp
from jax import lax
from jax.experimental import pallas as pl
from jax.experimental.pallas import tpu as pltpu

F32 = jnp.float32
BF16 = jnp.bfloat16

NORM_EPS = 1e-6
CHUNK = 64
TOPK_MAX = 256
A_HEADS = 8
HEAD_DIM = 128
IDX_HEADS = 16
IDX_DIM = 64
B_HEADS = 8
PEER_HEADS = 8
PEER_N_KEYS = 128
PEER_TOPK = 16
SUB = 16

VMEM_LIMIT = 56 * 1024 * 1024
INT_MIN = -2 ** 31

NT_DIMS = (((1,), (1,)), ((), ()))
TN_DIMS = (((0,), (0,)), ((), ()))


def _sigmoid(x):
    return 1.0 / (1.0 + jnp.exp(-x))


def _params(sem):
    return pltpu.CompilerParams(dimension_semantics=sem, vmem_limit_bytes=VMEM_LIMIT)


def _ada_kernel(c_ref, w_ref, b_ref, o_ref):
    c = c_ref[...]
    cond = (c * _sigmoid(c)).astype(BF16)
    o_ref[...] = jnp.dot(cond, w_ref[...].astype(BF16), preferred_element_type=F32) + b_ref[...]


def _ada(c, w, b, tn=1024):
    bsz, d = c.shape
    n = w.shape[1]
    return pl.pallas_call(
        _ada_kernel,
        out_shape=jax.ShapeDtypeStruct((bsz, n), F32),
        grid=(n // tn,),
        in_specs=[pl.BlockSpec((bsz, d), lambda j: (0, 0)),
                  pl.BlockSpec((d, tn), lambda j: (0, j)),
                  pl.BlockSpec((1, tn), lambda j: (0, j))],
        out_specs=pl.BlockSpec((bsz, tn), lambda j: (0, j)),
        compiler_params=_params(("arbitrary",)),
        name="ada",
    )(c, w, b.reshape(1, n))


def _norm_mod(x, gain, scale, shift):
    ms = jnp.mean(x * x, axis=-1, keepdims=True)
    y = x * lax.rsqrt(ms + NORM_EPS) * gain
    return y * (1.0 + scale) + shift


def _inproj_kernel(x_ref, g_ref, sc_ref, sh_ref, w_ref, o_ref, h_ref):
    @pl.when(pl.program_id(1) == 0)
    def _():
        h_ref[...] = _norm_mod(x_ref[...], g_ref[...], sc_ref[0], sh_ref[0]).astype(BF16)

    o_ref[...] = jnp.dot(h_ref[...], w_ref[...], preferred_element_type=F32).astype(o_ref.dtype)


def _inproj(x2d, gain, scale, shift, w, out_dtype, seq, tm, tn):
    t, d = x2d.shape
    n = w.shape[1]
    per_b = seq // tm
    return pl.pallas_call(
        _inproj_kernel,
        out_shape=jax.ShapeDtypeStruct((t, n), out_dtype),
        grid=(t // tm, n // tn),
        in_specs=[pl.BlockSpec((tm, d), lambda i, j: (i, 0)),
                  pl.BlockSpec((1, d), lambda i, j: (0, 0)),
                  pl.BlockSpec((1, 1, d), lambda i, j: (i // per_b, 0, 0)),
                  pl.BlockSpec((1, 1, d), lambda i, j: (i // per_b, 0, 0)),
                  pl.BlockSpec((d, tn), lambda i, j: (0, j))],
        out_specs=pl.BlockSpec((tm, tn), lambda i, j: (i, j)),
        scratch_shapes=[pltpu.VMEM((tm, d), BF16)],
        compiler_params=_params(("arbitrary", "arbitrary")),
        name="inproj",
    )(x2d, gain.reshape(1, d), scale, shift, w)


DSA_QB = 256
DSA_KT = 256
NEG_BIG = -1e30


def _dsa_kernel(qi_ref, kw_ref, qa_ref, ka_ref, va_ref, o_ref,
                key_ref, vt_ref, m_ref, l_ref, acc_ref):
    j = pl.program_id(1)
    qb, kt_rows = DSA_QB, DSA_KT
    seq = ka_ref.shape[0]
    n_kt = (j * qb + qb + kt_rows - 1) // kt_rows
    r0 = pl.multiple_of(j * qb, qb)

    @pl.when(j == 0)
    def _():
        for h in range(A_HEADS):
            for t in range(seq // kt_rows):
                blk = va_ref[t * kt_rows:(t + 1) * kt_rows, h * HEAD_DIM:(h + 1) * HEAD_DIM]
                vt_ref[h * HEAD_DIM:(h + 1) * HEAD_DIM, t * kt_rows:(t + 1) * kt_rows] = (
                    blk.astype(F32).T.astype(BF16))

    w_t = kw_ref[pl.ds(r0, qb), :].T
    q_chunk = (r0 + lax.broadcasted_iota(jnp.int32, (kt_rows, qb), 1)) // CHUNK
    key_iota = lax.broadcasted_iota(jnp.int32, (kt_rows, qb), 0)

    def score_tile(kt, carry):
        k0 = pl.multiple_of(kt * kt_rows, kt_rows)
        ki = kw_ref[pl.ds(k0, kt_rows), 0:IDX_DIM].astype(BF16)
        sc = jnp.zeros((kt_rows, qb), F32)
        for h in range(IDX_HEADS):
            dots = lax.dot_general(ki, qi_ref[:, h * IDX_DIM:(h + 1) * IDX_DIM], NT_DIMS,
                                   preferred_element_type=F32)
            sc = sc + w_t[IDX_DIM + h:IDX_DIM + h + 1, :] * jnp.maximum(dots, 0.0)
        sc = sc * ((IDX_DIM * IDX_HEADS) ** -0.5)
        bits = pltpu.bitcast(sc, jnp.int32)
        skey = bits ^ ((bits >> 31) & jnp.int32(0x7FFFFFFF))
        adm = (k0 + key_iota) // CHUNK <= q_chunk
        key_ref[pl.ds(k0, kt_rows), :] = jnp.where(adm, skey, jnp.int32(INT_MIN))
        return carry

    lax.fori_loop(0, n_kt, score_tile, 0)

    top_k = float(min(TOPK_MAX, seq // 4))

    def bit_step(it, v):
        cand = v + lax.shift_left(jnp.int32(1), 31 - it)

        def count_tile(kt, cnt):
            k0 = pl.multiple_of(kt * kt_rows, kt_rows)
            hit = jnp.where(key_ref[pl.ds(k0, kt_rows), :] >= cand, 1.0, 0.0)
            return cnt + jnp.sum(hit, axis=0, keepdims=True)

        cnt = lax.fori_loop(0, n_kt, count_tile, jnp.zeros((1, qb), F32))
        return jnp.where(cnt >= top_k, cand, v)

    v = lax.fori_loop(0, 32, bit_step, jnp.full((1, qb), INT_MIN, jnp.int32))
    v = jnp.maximum(v, jnp.int32(INT_MIN + 1))

    att_scale = HEAD_DIM ** -0.5
    m_ref[...] = jnp.full(m_ref.shape, NEG_BIG, F32)
    l_ref[...] = jnp.zeros(l_ref.shape, F32)
    acc_ref[...] = jnp.zeros(acc_ref.shape, F32)

    def att_tile(kt, carry):
        k0 = pl.multiple_of(kt * kt_rows, kt_rows)
        bias = jnp.where(key_ref[pl.ds(k0, kt_rows), :] >= v, 0.0, NEG_BIG)
        for h in range(A_HEADS):
            cols = slice(h * HEAD_DIM, (h + 1) * HEAD_DIM)
            st = lax.dot_general(ka_ref[pl.ds(k0, kt_rows), cols], qa_ref[:, cols], NT_DIMS,
                                 preferred_element_type=F32) * att_scale + bias
            m_old = m_ref[h:h + 1, :]
            m_new = jnp.maximum(m_old, jnp.max(st, axis=0, keepdims=True))
            alpha = jnp.exp(m_old - m_new)
            p = jnp.exp(st - m_new)
            m_ref[h:h + 1, :] = m_new
            l_ref[h:h + 1, :] = l_ref[h:h + 1, :] * alpha + jnp.sum(p, axis=0, keepdims=True)
            acc_ref[cols, :] = acc_ref[cols, :] * alpha + jnp.dot(
                vt_ref[cols, pl.ds(k0, kt_rows)], p.astype(BF16), preferred_element_type=F32)
        return carry

    lax.fori_loop(0, n_kt, att_tile, 0)
    for h in range(A_HEADS):
        cols = slice(h * HEAD_DIM, (h + 1) * HEAD_DIM)
        o_ref[:, cols] = (acc_ref[cols, :] / l_ref[h:h + 1, :]).T.astype(o_ref.dtype)


def _dsa(p1, p2, bsz, seq, col):
    n_blk = seq // DSA_QB
    width = A_HEADS * HEAD_DIM
    return pl.pallas_call(
        _dsa_kernel,
        out_shape=jax.ShapeDtypeStruct((bsz * seq, width), BF16),
        grid=(bsz, n_blk),
        in_specs=[pl.BlockSpec((DSA_QB, width), lambda b, k: (b * n_blk + k, col["qi"])),
                  pl.BlockSpec((seq, 128), lambda b, k: (b, col["kw"])),
                  pl.BlockSpec((DSA_QB, width), lambda b, k: (b * n_blk + k, col["qa"])),
                  pl.BlockSpec((seq, width), lambda b, k: (b, col["ka"])),
                  pl.BlockSpec((seq, width), lambda b, k: (b, col["va"]))],
        out_specs=pl.BlockSpec((DSA_QB, width), lambda b, k: (b * n_blk + k, 0)),
        scratch_shapes=[pltpu.VMEM((seq, DSA_QB), jnp.int32), pltpu.VMEM((width, seq), BF16),
                        pltpu.VMEM((A_HEADS, DSA_QB), F32), pltpu.VMEM((A_HEADS, DSA_QB), F32),
                        pltpu.VMEM((width, DSA_QB), F32)],
        compiler_params=_params(("arbitrary", "arbitrary")),
        name="dsa",
    )(p1, p2, p1, p1, p1)


def _hgrn_kernel(q_ref, f_ref, i_ref, g_ref, lb_ref, gain_ref, o_ref):
    c = CHUNK
    n_c = q_ref.shape[0] // c
    d = q_ref.shape[1]
    lb = lb_ref[0]
    gain = gain_ref[...]
    row = lax.broadcasted_iota(jnp.int32, (c, c), 0)
    colm = lax.broadcasted_iota(jnp.int32, (c, c), 1)
    tri = jnp.where(colm <= row, 1.0, 0.0).astype(F32)
    sub_row = lax.broadcasted_iota(jnp.int32, (SUB, d), 0)

    def chunk(ci, st_t):
        r0 = pl.multiple_of(ci * c, c)
        f = lb + (1.0 - lb) * _sigmoid(f_ref[pl.ds(r0, c), :])
        logf = jnp.log(f)
        k = 1.0 - f
        qr = q_ref[pl.ds(r0, c), :].astype(F32)
        q = qr * _sigmoid(qr)
        v = i_ref[pl.ds(r0, c), :].astype(F32)
        v16 = v.astype(BF16)
        b = jnp.dot(tri, logf, preferred_element_type=F32, precision=lax.Precision.HIGHEST)
        b_last = b[c - 1:c, :]

        o_inter = lax.dot_general((q * jnp.exp(b)).astype(BF16), st_t.astype(BF16), NT_DIMS,
                                  preferred_element_type=F32)
        outs = []
        for blk in range(c // SUB):
            lo = blk * SUB
            q_i, k_i, v_i, b_i = q[lo:lo + SUB], k[lo:lo + SUB], v[lo:lo + SUB], b[lo:lo + SUB]
            o_i = o_inter[lo:lo + SUB]
            if blk > 0:
                b_ref = b[lo - 1:lo, :]
                qt = (q_i * jnp.exp(b_i - b_ref)).astype(BF16)
                kt = (k[:lo] * jnp.exp(b_ref - b[:lo])).astype(BF16)
                att = lax.dot_general(qt, kt, NT_DIMS, preferred_element_type=F32)
                o_i = o_i + jnp.dot(att.astype(BF16), v16[:lo], preferred_element_type=F32)
            for s_ in range(SUB):
                w = jnp.exp(jnp.where(sub_row >= s_, b_i - b_i[s_:s_ + 1, :], -jnp.inf))
                a = jnp.sum(q_i * k_i[s_:s_ + 1, :] * w, axis=1, keepdims=True)
                o_i = o_i + a * v_i[s_:s_ + 1, :]
            outs.append(o_i)
        o = jnp.concatenate(outs, axis=0)

        ms = jnp.mean(o * o, axis=-1, keepdims=True)
        rec = o * lax.rsqrt(ms + NORM_EPS) * gain
        gr = g_ref[pl.ds(r0, c), :].astype(F32)
        o_ref[pl.ds(r0, c), :] = (rec * (gr * _sigmoid(gr))).astype(o_ref.dtype)

        k_end = (k * jnp.exp(b_last - b)).astype(BF16)
        upd = lax.dot_general(v16, k_end, TN_DIMS, preferred_element_type=F32)
        return st_t * jnp.exp(b_last) + upd

    lax.fori_loop(0, n_c, chunk, jnp.zeros((d, d), F32))


def _hgrn(p1, p2, lb, gain, bsz, seq, col):
    d = HEAD_DIM
    return pl.pallas_call(
        _hgrn_kernel,
        out_shape=jax.ShapeDtypeStruct((bsz * seq, B_HEADS * d), BF16),
        grid=(bsz, B_HEADS),
        in_specs=[pl.BlockSpec((seq, d), lambda b, h: (b, col["qb"] + h)),
                  pl.BlockSpec((seq, d), lambda b, h: (b, col["fb"] + h)),
                  pl.BlockSpec((seq, d), lambda b, h: (b, col["ib"] + h)),
                  pl.BlockSpec((seq, d), lambda b, h: (b, col["gb"] + h)),
                  pl.BlockSpec((1, 1, d), lambda b, h: (h, 0, 0)),
                  pl.BlockSpec((1, d), lambda b, h: (0, 0))],
        out_specs=pl.BlockSpec((seq, d), lambda b, h: (b, h)),
        compiler_params=_params(("arbitrary", "arbitrary")),
        name="hgrn",
    )(p1, p2, p1, p1, lb.reshape(B_HEADS, 1, d), gain.reshape(1, d))


def _mix_kernel(attn_ref, rec_ref, ga_ref, gb_ref, x_ref, gate_ref, wa_ref, wb_ref, wo_ref,
                g2_ref, sc_ref, sh_ref, wq_ref, x1_ref, h2_ref, qp_ref):
    ya = jnp.dot(attn_ref[...], wa_ref[...], preferred_element_type=F32)
    yb = jnp.dot(rec_ref[...], wb_ref[...], preferred_element_type=F32)
    mixed = _sigmoid(ga_ref[...].astype(F32)) * ya + _sigmoid(gb_ref[...].astype(F32)) * yb
    x1 = x_ref[...] + gate_ref[0] * jnp.dot(mixed.astype(BF16), wo_ref[...],
                                            preferred_element_type=F32)
    x1_ref[...] = x1
    h2 = _norm_mod(x1, g2_ref[...], sc_ref[0], sh_ref[0]).astype(BF16)
    h2_ref[...] = h2
    qp_ref[...] = jnp.dot(h2, wq_ref[...], preferred_element_type=F32).astype(qp_ref.dtype)


def _mix(attn, rec, p1, x2d, gate1, wa, wb, wo, g2, scale2, shift2, wq, seq, tm, col):
    t, d = x2d.shape
    wa_w = attn.shape[1]
    per_b = seq // tm
    full = lambda shape: pl.BlockSpec(shape, lambda i: (0,) * len(shape),
                                      pipeline_mode=pl.Buffered(1))
    mod = pl.BlockSpec((1, 1, d), lambda i: (i // per_b, 0, 0))
    return pl.pallas_call(
        _mix_kernel,
        out_shape=(jax.ShapeDtypeStruct((t, d), F32), jax.ShapeDtypeStruct((t, d), BF16),
                   jax.ShapeDtypeStruct((t, wq.shape[1]), BF16)),
        grid=(t // tm,),
        in_specs=[pl.BlockSpec((tm, wa_w), lambda i: (i, 0)),
                  pl.BlockSpec((tm, wa_w), lambda i: (i, 0)),
                  pl.BlockSpec((tm, d), lambda i: (i, col["g_a"])),
                  pl.BlockSpec((tm, d), lambda i: (i, col["g_b"])),
                  pl.BlockSpec((tm, d), lambda i: (i, 0)),
                  mod, full(wa.shape), full(wb.shape), full(wo.shape),
                  full((1, d)), mod, mod, full(wq.shape)],
        out_specs=(pl.BlockSpec((tm, d), lambda i: (i, 0)),
                   pl.BlockSpec((tm, d), lambda i: (i, 0)),
                   pl.BlockSpec((tm, wq.shape[1]), lambda i: (i, 0))),
        compiler_params=_params(("arbitrary",)),
        name="mix",
    )(attn, rec, p1, p1, x2d, gate1, wa, wb, wo, g2.reshape(1, d), scale2, shift2, wq)


def _top_values(work, n, dst_ref):
    for r in range(n):
        mx = jnp.max(work, axis=0, keepdims=True)
        dst_ref[r:r + 1, :] = mx
        work = jnp.where(work == mx, -jnp.inf, work)


def _gelu(x):
    return 0.5 * x * (1.0 + lax.erf(x * (2.0 ** -0.5)))


def _peer_kernel(qp_ref, keys_ref, h2_ref, u_ref, vt_ref, x1_ref, gate_ref, fin_ref, o_ref,
                 s0_ref, s1_ref, c0_ref, e1_ref, thr_ref, act_ref, acc_ref,
                 top_ref, cand_ref, best_ref):
    eb = pl.program_id(1)
    n_eb = pl.num_programs(1)
    nk = PEER_N_KEYS
    rows_per_step = u_ref.shape[0] // nk
    tm = h2_ref.shape[0]

    @pl.when(eb == 0)
    def _():
        acc_ref[...] = jnp.zeros_like(acc_ref)
        kk = PEER_TOPK
        for p in range(PEER_HEADS):
            for half, s_ref in ((0, s0_ref), (1, s1_ref)):
                j = 2 * p + half
                qs = qp_ref[:, j * nk:(j + 1) * nk]
                st = lax.dot_general(keys_ref[j], qs, NT_DIMS, preferred_element_type=F32)
                s_ref[p] = st
                _top_values(st, kk, top_ref.at[half])
            for a in range(kk):
                cand_ref[a * kk:(a + 1) * kk, :] = top_ref[0, a:a + 1, :] + top_ref[1]
            _top_values(cand_ref[...], kk, best_ref)
            best = best_ref[...]
            z = jnp.sum(jnp.exp(best - best[0:1, :]), axis=0, keepdims=True)
            thr_ref[p] = best[kk - 1:kk, :]
            c0_ref[p] = jnp.exp(s0_ref[p] - top_ref[0, 0:1, :]) / z
            e1_ref[p] = jnp.exp(s1_ref[p] - top_ref[1, 0:1, :])

    a_t = lax.dot_general(u_ref[...], h2_ref[...], NT_DIMS, preferred_element_type=F32)
    for r in range(rows_per_step):
        i = eb * rows_per_step + r
        g = jnp.zeros((nk, tm), F32)
        for p in range(PEER_HEADS):
            ssum = s0_ref[p, pl.ds(i, 1), :] + s1_ref[p]
            val = c0_ref[p, pl.ds(i, 1), :] * e1_ref[p]
            g = g + jnp.where(ssum >= thr_ref[p], val, 0.0)
        act_ref[r * nk:(r + 1) * nk, :] = (_gelu(a_t[r * nk:(r + 1) * nk, :]) * g).astype(BF16)
    acc_ref[...] += jnp.dot(vt_ref[...], act_ref[...], preferred_element_type=F32)

    @pl.when(eb == n_eb - 1)
    def _():
        x2 = x1_ref[...] + gate_ref[0] * acc_ref[...].T
        ms = jnp.mean(x2 * x2, axis=-1, keepdims=True)
        o_ref[...] = x2 * lax.rsqrt(ms + NORM_EPS) * fin_ref[...]


def _peer(qp, keys, h2, u, vt, x1, gate2, fin, seq, tm, te):
    t, d = x1.shape
    n_exp = u.shape[0]
    per_b = seq // tm
    nk = PEER_N_KEYS
    stat = pltpu.VMEM((PEER_HEADS, nk, tm), F32)
    return pl.pallas_call(
        _peer_kernel,
        out_shape=jax.ShapeDtypeStruct((t, d), F32),
        grid=(t // tm, n_exp // te),
        in_specs=[pl.BlockSpec((tm, qp.shape[1]), lambda i, e: (i, 0)),
                  pl.BlockSpec(keys.shape, lambda i, e: (0, 0, 0)),
                  pl.BlockSpec((tm, d), lambda i, e: (i, 0)),
                  pl.BlockSpec((te, d), lambda i, e: (e, 0)),
                  pl.BlockSpec((d, te), lambda i, e: (0, e)),
                  pl.BlockSpec((tm, d), lambda i, e: (i, 0)),
                  pl.BlockSpec((1, 1, d), lambda i, e: (i // per_b, 0, 0)),
                  pl.BlockSpec((1, d), lambda i, e: (0, 0))],
        out_specs=pl.BlockSpec((tm, d), lambda i, e: (i, 0)),
        scratch_shapes=[stat, stat, stat, stat,
                        pltpu.VMEM((PEER_HEADS, 1, tm), F32),
                        pltpu.VMEM((te, tm), BF16),
                        pltpu.VMEM((d, tm), F32),
                        pltpu.VMEM((2, PEER_TOPK, tm), F32),
                        pltpu.VMEM((PEER_TOPK * PEER_TOPK, tm), F32),
                        pltpu.VMEM((PEER_TOPK, tm), F32)],
        compiler_params=_params(("arbitrary", "arbitrary")),
        name="peer",
    )(qp, keys, h2, u, vt, x1, gate2, fin.reshape(1, d))


def kernel(x, c, w_ada, b_ada, norm_mix, norm_ffn, w_in, lb_logits, hgrn_gain, w_up_a, w_up_b,
           w_out, peer_w_q, peer_keys, peer_u, peer_v, final_norm):
    bsz, seq, d = x.shape
    depth = w_ada.shape[0]
    assert depth == 1, "the PEER kernel applies the final norm, so it must be the last layer"
    aw = A_HEADS * HEAD_DIM
    bw = B_HEADS * HEAD_DIM
    iw = IDX_HEADS * IDX_DIM
    lower_bounds = jnp.cumsum(jax.nn.softmax(lb_logits.astype(F32), axis=0), axis=0)
    xt = x.reshape(bsz * seq, d)

    widths = (aw, aw, aw, iw, IDX_DIM, IDX_HEADS, bw, bw, bw, bw, d, d)
    names = ("qa", "ka", "va", "qi", "ki", "wi", "qb", "fb", "ib", "gb", "g_a", "g_b")
    off, start = {}, 0
    for nme, wd in zip(names, widths):
        off[nme] = (start, start + wd)
        start += wd
    order1 = ("g_a", "g_b", "qa", "ka", "va", "qi", "qb", "ib", "gb")
    order2 = ("fb", "ki", "wi")
    pad2 = 128 - IDX_DIM - IDX_HEADS

    for l in range(depth):
        mod = _ada(c, w_ada[l], b_ada[l])
        shift1, scale1, gate1, shift2, scale2, gate2 = [
            m.reshape(bsz, 1, d) for m in jnp.split(mod, 6, axis=-1)]

        wl = w_in[l]
        w1 = jnp.concatenate([wl[:, off[n][0]:off[n][1]] for n in order1], axis=1).astype(BF16)
        w2 = jnp.concatenate([wl[:, off[n][0]:off[n][1]] for n in order2]
                             + [jnp.zeros((d, pad2), wl.dtype)], axis=1).astype(BF16)
        p1 = _inproj(xt, norm_mix[l], scale1, shift1, w1, BF16, seq, tm=1024, tn=1024)
        p2 = _inproj(xt, norm_mix[l], scale1, shift1, w2, F32, seq, tm=1024, tn=w2.shape[1])

        attn = _dsa(p1, p2, bsz, seq, {"qa": 4, "ka": 5, "va": 6, "qi": 7, "kw": 8})
        rec = _hgrn(p1, p2, lower_bounds[l], hgrn_gain[l], bsz, seq,
                    {"qb": 64, "ib": 72, "gb": 80, "fb": 0})

        x1, h2, qp = _mix(attn, rec, p1, xt, gate1,
                          w_up_a[l].astype(BF16), w_up_b[l].astype(BF16), w_out[l].astype(BF16),
                          norm_ffn[l], scale2, shift2, peer_w_q[l].astype(BF16),
                          seq, tm=256, col={"g_a": 0, "g_b": 1})
        keys = peer_keys[l].reshape(2 * PEER_HEADS, PEER_N_KEYS, -1).astype(BF16)
        xt = _peer(qp, keys, h2, peer_u[l].astype(BF16), peer_v[l].T.astype(BF16), x1, gate2,
                   final_norm, seq, tm=512, te=512)
    return xt.reshape(bsz, seq, d)
```

```python
import functools

import jax
import jax.numpy as jnp
from jax import lax
from jax.experimental import pallas as pl
from jax.experimental.pallas import tpu as pltpu

F32 = jnp.float32
BF16 = jnp.bfloat16

NORM_EPS = 1e-6
CHUNK = 64
TOPK_MAX = 256
A_HEADS = 8
HEAD_DIM = 128
IDX_HEADS = 16
IDX_DIM = 64
B_HEADS = 8
PEER_HEADS = 8
PEER_N_KEYS = 128
PEER_TOPK = 16
SUB = 16

VMEM_LIMIT = 56 * 1024 * 1024
INT_MIN = -2 ** 31

NT_DIMS = (((1,), (1,)), ((), ()))
TN_DIMS = (((0,), (0,)), ((), ()))


def _sigmoid(x):
    return 1.0 / (1.0 + jnp.exp(-x))


def _params(sem):
    return pltpu.CompilerParams(dimension_semantics=sem, vmem_limit_bytes=VMEM_LIMIT)


def _ada_kernel(c_ref, w_ref, b_ref, o_ref):
    c = c_ref[...]
    cond = (c * _sigmoid(c)).astype(BF16)
    o_ref[...] = jnp.dot(cond, w_ref[...].astype(BF16), preferred_element_type=F32) + b_ref[...]


def _ada(c, w, b, tn=1024):
    bsz, d = c.shape
    n = w.shape[1]
    return pl.pallas_call(
        _ada_kernel,
        out_shape=jax.ShapeDtypeStruct((bsz, n), F32),
        grid=(n // tn,),
        in_specs=[pl.BlockSpec((bsz, d), lambda j: (0, 0)),
                  pl.BlockSpec((d, tn), lambda j: (0, j)),
                  pl.BlockSpec((1, tn), lambda j: (0, j))],
        out_specs=pl.BlockSpec((bsz, tn), lambda j: (0, j)),
        compiler_params=_params(("arbitrary",)),
        name="ada",
    )(c, w, b.reshape(1, n))


def _norm_mod(x, gain, scale, shift):
    ms = jnp.mean(x * x, axis=-1, keepdims=True)
    y = x * lax.rsqrt(ms + NORM_EPS) * gain
    return y * (1.0 + scale) + shift


def _inproj_kernel(x_ref, g_ref, sc_ref, sh_ref, w_ref, o_ref, h_ref):
    @pl.when(pl.program_id(1) == 0)
    def _():
        h_ref[...] = _norm_mod(x_ref[...], g_ref[...], sc_ref[0], sh_ref[0]).astype(BF16)

    o_ref[...] = jnp.dot(h_ref[...], w_ref[...], preferred_element_type=F32).astype(o_ref.dtype)


def _inproj(x2d, gain, scale, shift, w, out_dtype, seq, tm, tn):
    t, d = x2d.shape
    n = w.shape[1]
    per_b = seq // tm
    return pl.pallas_call(
        _inproj_kernel,
        out_shape=jax.ShapeDtypeStruct((t, n), out_dtype),
        grid=(t // tm, n // tn),
        in_specs=[pl.BlockSpec((tm, d), lambda i, j: (i, 0)),
                  pl.BlockSpec((1, d), lambda i, j: (0, 0)),
                  pl.BlockSpec((1, 1, d), lambda i, j: (i // per_b, 0, 0)),
                  pl.BlockSpec((1, 1, d), lambda i, j: (i // per_b, 0, 0)),
                  pl.BlockSpec((d, tn), lambda i, j: (0, j))],
        out_specs=pl.BlockSpec((tm, tn), lambda i, j: (i, j)),
        scratch_shapes=[pltpu.VMEM((tm, d), BF16)],
        compiler_params=_params(("arbitrary", "arbitrary")),
        name="inproj",
    )(x2d, gain.reshape(1, d), scale, shift, w)


DSA_QB = 256
DSA_KT = 256
NEG_BIG = -1e30


def _dsa_kernel(qi_ref, kw_ref, qa_ref, ka_ref, va_ref, o_ref,
                key_ref, vt_ref, m_ref, l_ref, acc_ref):
    j = pl.program_id(1)
    qb, kt_rows = DSA_QB, DSA_KT
    seq = ka_ref.shape[0]
    n_kt = (j * qb + qb + kt_rows - 1) // kt_rows
    r0 = pl.multiple_of(j * qb, qb)

    @pl.when(j == 0)
    def _():
        for h in range(A_HEADS):
            for t in range(seq // kt_rows):
                blk = va_ref[t * kt_rows:(t + 1) * kt_rows, h * HEAD_DIM:(h + 1) * HEAD_DIM]
                vt_ref[h * HEAD_DIM:(h + 1) * HEAD_DIM, t * kt_rows:(t + 1) * kt_rows] = (
                    blk.astype(F32).T.astype(BF16))

    w_t = kw_ref[pl.ds(r0, qb), :].T
    q_chunk = (r0 + lax.broadcasted_iota(jnp.int32, (kt_rows, qb), 1)) // CHUNK
    key_iota = lax.broadcasted_iota(jnp.int32, (kt_rows, qb), 0)

    def score_tile(kt, carry):
        k0 = pl.multiple_of(kt * kt_rows, kt_rows)
        ki = kw_ref[pl.ds(k0, kt_rows), 0:IDX_DIM].astype(BF16)
        sc = jnp.zeros((kt_rows, qb), F32)
        for h in range(IDX_HEADS):
            dots = lax.dot_general(ki, qi_ref[:, h * IDX_DIM:(h + 1) * IDX_DIM], NT_DIMS,
                                   preferred_element_type=F32)
            sc = sc + w_t[IDX_DIM + h:IDX_DIM + h + 1, :] * jnp.maximum(dots, 0.0)
        sc = sc * ((IDX_DIM * IDX_HEADS) ** -0.5)
        bits = pltpu.bitcast(sc, jnp.int32)
        skey = bits ^ ((bits >> 31) & jnp.int32(0x7FFFFFFF))
        adm = (k0 + key_iota) // CHUNK <= q_chunk
        key_ref[pl.ds(k0, kt_rows), :] = jnp.where(adm, skey, jnp.int32(INT_MIN))
        return carry

    lax.fori_loop(0, n_kt, score_tile, 0)

    top_k = float(min(TOPK_MAX, seq // 4))

    def bit_step(it, v):
        cand = v + lax.shift_left(jnp.int32(1), 31 - it)

        def count_tile(kt, cnt):
            k0 = pl.multiple_of(kt * kt_rows, kt_rows)
            hit = jnp.where(key_ref[pl.ds(k0, kt_rows), :] >= cand, 1.0, 0.0)
            return cnt + jnp.sum(hit, axis=0, keepdims=True)

        cnt = lax.fori_loop(0, n_kt, count_tile, jnp.zeros((1, qb), F32))
        return jnp.where(cnt >= top_k, cand, v)

    v = lax.fori_loop(0, 32, bit_step, jnp.full((1, qb), INT_MIN, jnp.int32))
    v = jnp.maximum(v, jnp.int32(INT_MIN + 1))

    att_scale = HEAD_DIM ** -0.5
    m_ref[...] = jnp.full(m_ref.shape, NEG_BIG, F32)
    l_ref[...] = jnp.zeros(l_ref.shape, F32)
    acc_ref[...] = jnp.zeros(acc_ref.shape, F32)

    def att_tile(kt, carry):
        k0 = pl.multiple_of(kt * kt_rows, kt_rows)
        bias = jnp.where(key_ref[pl.ds(k0, kt_rows), :] >= v, 0.0, NEG_BIG)
        for h in range(A_HEADS):
            cols = slice(h * HEAD_DIM, (h + 1) * HEAD_DIM)
            st = lax.dot_general(ka_ref[pl.ds(k0, kt_rows), cols], qa_ref[:, cols], NT_DIMS,
                                 preferred_element_type=F32) * att_scale + bias
            m_old = m_ref[h:h + 1, :]
            m_new = jnp.maximum(m_old, jnp.max(st, axis=0, keepdims=True))
            alpha = jnp.exp(m_old - m_new)
            p = jnp.exp(st - m_new)
            m_ref[h:h + 1, :] = m_new
            l_ref[h:h + 1, :] = l_ref[h:h + 1, :] * alpha + jnp.sum(p, axis=0, keepdims=True)
            acc_ref[cols, :] = acc_ref[cols, :] * alpha + jnp.dot(
                vt_ref[cols, pl.ds(k0, kt_rows)], p.astype(BF16), preferred_element_type=F32)
        return carry

    lax.fori_loop(0, n_kt, att_tile, 0)
    for h in range(A_HEADS):
        cols = slice(h * HEAD_DIM, (h + 1) * HEAD_DIM)
        o_ref[:, cols] = (acc_ref[cols, :] / l_ref[h:h + 1, :]).T.astype(o_ref.dtype)


def _dsa(p1, p2, bsz, seq, col):
    n_blk = seq // DSA_QB
    width = A_HEADS * HEAD_DIM
    return pl.pallas_call(
        _dsa_kernel,
        out_shape=jax.ShapeDtypeStruct((bsz * seq, width), BF16),
        grid=(bsz, n_blk),
        in_specs=[pl.BlockSpec((DSA_QB, width), lambda b, k: (b * n_blk + k, col["qi"])),
                  pl.BlockSpec((seq, 128), lambda b, k: (b, col["kw"])),
                  pl.BlockSpec((DSA_QB, width), lambda b, k: (b * n_blk + k, col["qa"])),
                  pl.BlockSpec((seq, width), lambda b, k: (b, col["ka"])),
                  pl.BlockSpec((seq, width), lambda b, k: (b, col["va"]))],
        out_specs=pl.BlockSpec((DSA_QB, width), lambda b, k: (b * n_blk + k, 0)),
        scratch_shapes=[pltpu.VMEM((seq, DSA_QB), jnp.int32), pltpu.VMEM((width, seq), BF16),
                        pltpu.VMEM((A_HEADS, DSA_QB), F32), pltpu.VMEM((A_HEADS, DSA_QB), F32),
                        pltpu.VMEM((width, DSA_QB), F32)],
        compiler_params=_params(("arbitrary", "arbitrary")),
        name="dsa",
    )(p1, p2, p1, p1, p1)


HGRN_GROUP = 4


def _hgrn_kernel(q_ref, f_ref, i_ref, g_ref, lb_ref, gain_ref, o_ref,
                 qe_ref, oi_ref, upd_ref, dec_ref):
    c = CHUNK
    n_c = q_ref.shape[0] // c
    d = q_ref.shape[1]
    lb = lb_ref[0]
    gain = gain_ref[...]
    row = lax.broadcasted_iota(jnp.int32, (c, c), 0)
    colm = lax.broadcasted_iota(jnp.int32, (c, c), 1)
    tri = jnp.where(colm <= row, 1.0, 0.0).astype(F32)
    sub_row = lax.broadcasted_iota(jnp.int32, (SUB, d), 0)

    def local_part(ci):
        r0 = pl.multiple_of(ci * c, c)
        f = lb + (1.0 - lb) * _sigmoid(f_ref[pl.ds(r0, c), :])
        logf = jnp.log(f)
        k = 1.0 - f
        qr = q_ref[pl.ds(r0, c), :].astype(F32)
        q = qr * _sigmoid(qr)
        v = i_ref[pl.ds(r0, c), :].astype(F32)
        v16 = v.astype(BF16)
        b = jnp.dot(tri, logf, preferred_element_type=F32, precision=lax.Precision.HIGHEST)
        b_last = b[c - 1:c, :]
        qe_ref[pl.ds(r0, c), :] = (q * jnp.exp(b)).astype(BF16)

        for blk in range(c // SUB):
            lo = blk * SUB
            q_i, k_i, v_i, b_i = q[lo:lo + SUB], k[lo:lo + SUB], v[lo:lo + SUB], b[lo:lo + SUB]
            o_i = jnp.zeros((SUB, d), F32)
            if blk > 0:
                b_ref = b[lo - 1:lo, :]
                qt = (q_i * jnp.exp(b_i - b_ref)).astype(BF16)
                kt = (k[:lo] * jnp.exp(b_ref - b[:lo])).astype(BF16)
                att = lax.dot_general(qt, kt, NT_DIMS, preferred_element_type=F32)
                o_i = jnp.dot(att.astype(BF16), v16[:lo], preferred_element_type=F32)
            for s_ in range(SUB):
                w = jnp.exp(jnp.where(sub_row >= s_, b_i - b_i[s_:s_ + 1, :], -jnp.inf))
                a = jnp.sum(q_i * k_i[s_:s_ + 1, :] * w, axis=1, keepdims=True)
                o_i = o_i + a * v_i[s_:s_ + 1, :]
            oi_ref[pl.ds(r0 + lo, SUB), :] = o_i

        k_end = (k * jnp.exp(b_last - b)).astype(BF16)
        upd_ref[ci] = lax.dot_general(v16, k_end, TN_DIMS, preferred_element_type=F32)
        dec_ref[ci] = jnp.exp(b_last)

    def group(gi, carry):
        for u in range(HGRN_GROUP):
            local_part(gi * HGRN_GROUP + u)
        return carry

    lax.fori_loop(0, n_c // HGRN_GROUP, group, 0)

    st_t = jnp.zeros((d, d), F32)
    for ci in range(n_c):
        rows = slice(ci * c, (ci + 1) * c)
        o = oi_ref[rows, :] + lax.dot_general(qe_ref[rows, :], st_t.astype(BF16), NT_DIMS,
                                              preferred_element_type=F32)
        ms = jnp.mean(o * o, axis=-1, keepdims=True)
        rec = o * lax.rsqrt(ms + NORM_EPS) * gain
        gr = g_ref[rows, :].astype(F32)
        o_ref[rows, :] = (rec * (gr * _sigmoid(gr))).astype(o_ref.dtype)
        st_t = st_t * dec_ref[ci] + upd_ref[ci]


def _hgrn(p1, p2, lb, gain, bsz, seq, col):
    d = HEAD_DIM
    n_c = seq // CHUNK
    return pl.pallas_call(
        _hgrn_kernel,
        out_shape=jax.ShapeDtypeStruct((bsz * seq, B_HEADS * d), BF16),
        grid=(bsz, B_HEADS),
        in_specs=[pl.BlockSpec((seq, d), lambda b, h: (b, col["qb"] + h)),
                  pl.BlockSpec((seq, d), lambda b, h: (b, col["fb"] + h)),
                  pl.BlockSpec((seq, d), lambda b, h: (b, col["ib"] + h)),
                  pl.BlockSpec((seq, d), lambda b, h: (b, col["gb"] + h)),
                  pl.BlockSpec((1, 1, d), lambda b, h: (h, 0, 0)),
                  pl.BlockSpec((1, d), lambda b, h: (0, 0))],
        out_specs=pl.BlockSpec((seq, d), lambda b, h: (b, h)),
        scratch_shapes=[pltpu.VMEM((seq, d), BF16), pltpu.VMEM((seq, d), F32),
                        pltpu.VMEM((n_c, d, d), F32), pltpu.VMEM((n_c, 1, d), F32)],
        compiler_params=_params(("arbitrary", "arbitrary")),
        name="hgrn",
    )(p1, p2, p1, p1, lb.reshape(B_HEADS, 1, d), gain.reshape(1, d))


def _mix_kernel(attn_ref, rec_ref, ga_ref, gb_ref, x_ref, gate_ref, wa_ref, wb_ref, wo_ref,
                g2_ref, sc_ref, sh_ref, wq_ref, x1_ref, h2_ref, qp_ref):
    ya = jnp.dot(attn_ref[...], wa_ref[...], preferred_element_type=F32)
    yb = jnp.dot(rec_ref[...], wb_ref[...], preferred_element_type=F32)
    mixed = _sigmoid(ga_ref[...].astype(F32)) * ya + _sigmoid(gb_ref[...].astype(F32)) * yb
    x1 = x_ref[...] + gate_ref[0] * jnp.dot(mixed.astype(BF16), wo_ref[...],
                                            preferred_element_type=F32)
    x1_ref[...] = x1
    h2 = _norm_mod(x1, g2_ref[...], sc_ref[0], sh_ref[0]).astype(BF16)
    h2_ref[...] = h2
    qp_ref[...] = jnp.dot(h2, wq_ref[...], preferred_element_type=F32).astype(qp_ref.dtype)


def _mix(attn, rec, p1, x2d, gate1, wa, wb, wo, g2, scale2, shift2, wq, seq, tm, col):
    t, d = x2d.shape
    wa_w = attn.shape[1]
    per_b = seq // tm
    full = lambda shape: pl.BlockSpec(shape, lambda i: (0,) * len(shape),
                                      pipeline_mode=pl.Buffered(1))
    mod = pl.BlockSpec((1, 1, d), lambda i: (i // per_b, 0, 0))
    return pl.pallas_call(
        _mix_kernel,
        out_shape=(jax.ShapeDtypeStruct((t, d), F32), jax.ShapeDtypeStruct((t, d), BF16),
                   jax.ShapeDtypeStruct((t, wq.shape[1]), BF16)),
        grid=(t // tm,),
        in_specs=[pl.BlockSpec((tm, wa_w), lambda i: (i, 0)),
                  pl.BlockSpec((tm, wa_w), lambda i: (i, 0)),
                  pl.BlockSpec((tm, d), lambda i: (i, col["g_a"])),
                  pl.BlockSpec((tm, d), lambda i: (i, col["g_b"])),
                  pl.BlockSpec((tm, d), lambda i: (i, 0)),
                  mod, full(wa.shape), full(wb.shape), full(wo.shape),
                  full((1, d)), mod, mod, full(wq.shape)],
        out_specs=(pl.BlockSpec((tm, d), lambda i: (i, 0)),
                   pl.BlockSpec((tm, d), lambda i: (i, 0)),
                   pl.BlockSpec((tm, wq.shape[1]), lambda i: (i, 0))),
        compiler_params=_params(("arbitrary",)),
        name="mix",
    )(attn, rec, p1, p1, x2d, gate1, wa, wb, wo, g2.reshape(1, d), scale2, shift2, wq)


def _top_values(work, n, dst_ref):
    for r in range(n):
        mx = jnp.max(work, axis=0, keepdims=True)
        dst_ref[r:r + 1, :] = mx
        work = jnp.where(work == mx, -jnp.inf, work)


def _gelu(x):
    return 0.5 * x * (1.0 + lax.erf(x * (2.0 ** -0.5)))


def _peer_stats(qp_ref, keys_ref, s0_ref, s1_ref, c0_ref, e1_ref, thr_ref,
                top_ref, cand_ref, best_ref):
    nk, kk = PEER_N_KEYS, PEER_TOPK
    row8 = lax.broadcasted_iota(jnp.int32, (8, cand_ref.shape[1]), 0)
    for p in range(PEER_HEADS):
        for half, s_ref in ((0, s0_ref), (1, s1_ref)):
            j = 2 * p + half
            qs = qp_ref[:, j * nk:(j + 1) * nk]
            st = lax.dot_general(keys_ref[j], qs, NT_DIMS, preferred_element_type=F32)
            s_ref[p] = st
            _top_values(st, kk, top_ref.at[half])
        cand_ref[0:kk, :] = top_ref[0, 0:1, :] + top_ref[1]
        cand_ref[kk:kk + 8, :] = top_ref[0, 1:2, :] + top_ref[1, 0:8, :]
        for a in range(2, 8):
            grp = top_ref[0, a:a + 1, :] + top_ref[1, 0:8, :]
            cand_ref[8 + 8 * a:16 + 8 * a, :] = jnp.where(row8 < kk // (a + 1), grp, -jnp.inf)
        cand_ref[72:80, :] = top_ref[0, 8:16, :] + top_ref[1, 0:1, :]
        _top_values(cand_ref[...], kk, best_ref)
        best = best_ref[...]
        z = jnp.sum(jnp.exp(best - best[0:1, :]), axis=0, keepdims=True)
        thr_ref[p] = best[kk - 1:kk, :]
        c0_ref[p] = jnp.exp(s0_ref[p] - top_ref[0, 0:1, :]) / z
        e1_ref[p] = jnp.exp(s1_ref[p] - top_ref[1, 0:1, :])


def _peer_kernel(qp_ref, keys_ref, h2_ref, u_ref, vt_ref, x1_ref, gate_ref, fin_ref, o_ref,
                 s0_ref, s1_ref, c0_ref, e1_ref, thr_ref, acc_ref, top_ref, cand_ref, best_ref,
                 act_ref):
    eb = pl.program_id(1)
    n_eb = pl.num_programs(1)
    nk = PEER_N_KEYS
    rows = u_ref.shape[0] // nk
    tm = h2_ref.shape[0]

    @pl.when(eb == 0)
    def _():
        acc_ref[...] = jnp.zeros_like(acc_ref)
        _peer_stats(qp_ref, keys_ref, s0_ref, s1_ref, c0_ref, e1_ref, thr_ref,
                    top_ref, cand_ref, best_ref)

    a_t = lax.dot_general(u_ref[...], h2_ref[...], NT_DIMS, preferred_element_type=F32)
    for r in range(rows):
        i = eb * rows + r
        gmat = jnp.zeros((nk, tm), F32)
        for p in range(PEER_HEADS):
            ssum = s0_ref[p, pl.ds(i, 1), :] + s1_ref[p]
            val = c0_ref[p, pl.ds(i, 1), :] * e1_ref[p]
            gmat = gmat + jnp.where(ssum >= thr_ref[p], val, 0.0)
        sl = slice(r * nk, (r + 1) * nk)
        act_ref[sl, :] = (_gelu(a_t[sl, :]) * gmat).astype(BF16)
    acc_ref[...] += jnp.dot(vt_ref[...], act_ref[...], preferred_element_type=F32)

    @pl.when(eb == n_eb - 1)
    def _():
        x2 = x1_ref[...] + gate_ref[0] * acc_ref[...].T
        ms = jnp.mean(x2 * x2, axis=-1, keepdims=True)
        o_ref[...] = x2 * lax.rsqrt(ms + NORM_EPS) * fin_ref[...]


def _peer(qp, keys, h2, u, vt, x1, gate2, fin, seq, tm, te):
    t, d = x1.shape
    per_b = seq // tm
    nk = PEER_N_KEYS
    stat = pltpu.VMEM((PEER_HEADS, nk, tm), F32)
    return pl.pallas_call(
        _peer_kernel,
        out_shape=jax.ShapeDtypeStruct((t, d), F32),
        grid=(t // tm, u.shape[0] // te),
        in_specs=[pl.BlockSpec((tm, qp.shape[1]), lambda i, e: (i, 0)),
                  pl.BlockSpec(keys.shape, lambda i, e: (0, 0, 0)),
                  pl.BlockSpec((tm, d), lambda i, e: (i, 0)),
                  pl.BlockSpec((te, d), lambda i, e: (e, 0)),
                  pl.BlockSpec((d, te), lambda i, e: (0, e)),
                  pl.BlockSpec((tm, d), lambda i, e: (i, 0)),
                  pl.BlockSpec((1, 1, d), lambda i, e: (i // per_b, 0, 0)),
                  pl.BlockSpec((1, d), lambda i, e: (0, 0))],
        out_specs=pl.BlockSpec((tm, d), lambda i, e: (i, 0)),
        scratch_shapes=[stat, stat, stat, stat,
                        pltpu.VMEM((PEER_HEADS, 1, tm), F32),
                        pltpu.VMEM((d, tm), F32),
                        pltpu.VMEM((2, PEER_TOPK, tm), F32),
                        pltpu.VMEM((80, tm), F32),
                        pltpu.VMEM((PEER_TOPK, tm), F32),
                        pltpu.VMEM((te, tm), BF16)],
        compiler_params=_params(("arbitrary", "arbitrary")),
        name="peer",
    )(qp, keys, h2, u, vt, x1, gate2, fin.reshape(1, d))


def kernel(x, c, w_ada, b_ada, norm_mix, norm_ffn, w_in, lb_logits, hgrn_gain, w_up_a, w_up_b,
           w_out, peer_w_q, peer_keys, peer_u, peer_v, final_norm):
    bsz, seq, d = x.shape
    depth = w_ada.shape[0]
    assert depth == 1, "the PEER kernel applies the final norm, so it must be the last layer"
    aw = A_HEADS * HEAD_DIM
    bw = B_HEADS * HEAD_DIM
    iw = IDX_HEADS * IDX_DIM
    lower_bounds = jnp.cumsum(jax.nn.softmax(lb_logits.astype(F32), axis=0), axis=0)
    xt = x.reshape(bsz * seq, d)

    widths = (aw, aw, aw, iw, IDX_DIM, IDX_HEADS, bw, bw, bw, bw, d, d)
    names = ("qa", "ka", "va", "qi", "ki", "wi", "qb", "fb", "ib", "gb", "g_a", "g_b")
    off, start = {}, 0
    for nme, wd in zip(names, widths):
        off[nme] = (start, start + wd)
        start += wd
    order1 = ("g_a", "g_b", "qa", "ka", "va", "qi", "qb", "ib", "gb")
    order2 = ("fb", "ki", "wi")
    pad2 = 128 - IDX_DIM - IDX_HEADS

    for l in range(depth):
        mod = _ada(c, w_ada[l], b_ada[l])
        shift1, scale1, gate1, shift2, scale2, gate2 = [
            m.reshape(bsz, 1, d) for m in jnp.split(mod, 6, axis=-1)]

        wl = w_in[l]
        w1 = jnp.concatenate([wl[:, off[n][0]:off[n][1]] for n in order1], axis=1).astype(BF16)
        w2 = jnp.concatenate([wl[:, off[n][0]:off[n][1]] for n in order2]
                             + [jnp.zeros((d, pad2), wl.dtype)], axis=1).astype(BF16)
        p1 = _inproj(xt, norm_mix[l], scale1, shift1, w1, BF16, seq, tm=1024, tn=1024)
        p2 = _inproj(xt, norm_mix[l], scale1, shift1, w2, F32, seq, tm=1024, tn=w2.shape[1])

        attn = _dsa(p1, p2, bsz, seq, {"qa": 4, "ka": 5, "va": 6, "qi": 7, "kw": 8})
        rec = _hgrn(p1, p2, lower_bounds[l], hgrn_gain[l], bsz, seq,
                    {"qb": 64, "ib": 72, "gb": 80, "fb": 0})

        x1, h2, qp = _mix(attn, rec, p1, xt, gate1,
                          w_up_a[l].astype(BF16), w_up_b[l].astype(BF16), w_out[l].astype(BF16),
                          norm_ffn[l], scale2, shift2, peer_w_q[l].astype(BF16),
                          seq, tm=256, col={"g_a": 0, "g_b": 1})
        keys = peer_keys[l].reshape(2 * PEER_HEADS, PEER_N_KEYS, -1).astype(BF16)
        xt = _peer(qp, keys, h2, peer_u[l].astype(BF16), peer_v[l].T.astype(BF16), x1, gate2,
                   final_norm, seq, tm=512, te=512)
    return xt.reshape(bsz, seq, d)
```

```python
import functools

import jax
import jax.numpy as jnp
from jax import lax
from jax.experimental import pallas as pl
from jax.experimental.pallas import tpu as pltpu

F32 = jnp.float32
BF16 = jnp.bfloat16

NORM_EPS = 1e-6
CHUNK = 64
TOPK_MAX = 256
A_HEADS = 8
HEAD_DIM = 128
IDX_HEADS = 16
IDX_DIM = 64
B_HEADS = 8
PEER_HEADS = 8
PEER_N_KEYS = 128
PEER_TOPK = 16
SUB = 16

VMEM_LIMIT = 56 * 1024 * 1024
INT_MIN = -2 ** 31

NT_DIMS = (((1,), (1,)), ((), ()))
TN_DIMS = (((0,), (0,)), ((), ()))


def _sigmoid(x):
    return 1.0 / (1.0 + jnp.exp(-x))


def _params(sem):
    return pltpu.CompilerParams(dimension_semantics=sem, vmem_limit_bytes=VMEM_LIMIT)


def _ada_kernel(c_ref, w_ref, b_ref, o_ref):
    c = c_ref[...]
    cond = (c * _sigmoid(c)).astype(BF16)
    o_ref[...] = jnp.dot(cond, w_ref[...].astype(BF16), preferred_element_type=F32) + b_ref[...]


def _ada(c, w, b, tn=1024):
    bsz, d = c.shape
    n = w.shape[1]
    return pl.pallas_call(
        _ada_kernel,
        out_shape=jax.ShapeDtypeStruct((bsz, n), F32),
        grid=(n // tn,),
        in_specs=[pl.BlockSpec((bsz, d), lambda j: (0, 0)),
                  pl.BlockSpec((d, tn), lambda j: (0, j)),
                  pl.BlockSpec((1, tn), lambda j: (0, j))],
        out_specs=pl.BlockSpec((bsz, tn), lambda j: (0, j)),
        compiler_params=_params(("arbitrary",)),
        name="ada",
    )(c, w, b.reshape(1, n))


def _norm_mod(x, gain, scale, shift):
    ms = jnp.mean(x * x, axis=-1, keepdims=True)
    y = x * lax.rsqrt(ms + NORM_EPS) * gain
    return y * (1.0 + scale) + shift


def _inproj_kernel(x_ref, g_ref, sc_ref, sh_ref, w_ref, o_ref, h_ref):
    @pl.when(pl.program_id(1) == 0)
    def _():
        h_ref[...] = _norm_mod(x_ref[...], g_ref[...], sc_ref[0], sh_ref[0]).astype(BF16)

    o_ref[...] = jnp.dot(h_ref[...], w_ref[...], preferred_element_type=F32).astype(o_ref.dtype)


def _inproj(x2d, gain, scale, shift, w, out_dtype, seq, tm, tn):
    t, d = x2d.shape
    n = w.shape[1]
    per_b = seq // tm
    return pl.pallas_call(
        _inproj_kernel,
        out_shape=jax.ShapeDtypeStruct((t, n), out_dtype),
        grid=(t // tm, n // tn),
        in_specs=[pl.BlockSpec((tm, d), lambda i, j: (i, 0)),
                  pl.BlockSpec((1, d), lambda i, j: (0, 0)),
                  pl.BlockSpec((1, 1, d), lambda i, j: (i // per_b, 0, 0)),
                  pl.BlockSpec((1, 1, d), lambda i, j: (i // per_b, 0, 0)),
                  pl.BlockSpec((d, tn), lambda i, j: (0, j))],
        out_specs=pl.BlockSpec((tm, tn), lambda i, j: (i, j)),
        scratch_shapes=[pltpu.VMEM((tm, d), BF16)],
        compiler_params=_params(("arbitrary", "arbitrary")),
        name="inproj",
    )(x2d, gain.reshape(1, d), scale, shift, w)


DSA_QB = 256
DSA_KT = 256
NEG_BIG = -1e30


def _dsa_kernel(qi_ref, kw_ref, qa_ref, ka_ref, va_ref, o_ref,
                key_ref, vt_ref, m_ref, l_ref, acc_ref):
    j = pl.program_id(1)
    qb, kt_rows = DSA_QB, DSA_KT
    seq = ka_ref.shape[0]
    n_kt = (j * qb + qb + kt_rows - 1) // kt_rows
    r0 = pl.multiple_of(j * qb, qb)

    @pl.when(j == 0)
    def _():
        for h in range(A_HEADS):
            for t in range(seq // kt_rows):
                blk = va_ref[t * kt_rows:(t + 1) * kt_rows, h * HEAD_DIM:(h + 1) * HEAD_DIM]
                vt_ref[h * HEAD_DIM:(h + 1) * HEAD_DIM, t * kt_rows:(t + 1) * kt_rows] = (
                    blk.astype(F32).T.astype(BF16))

    w_t = kw_ref[pl.ds(r0, qb), :].T
    q_chunk = (r0 + lax.broadcasted_iota(jnp.int32, (kt_rows, qb), 1)) // CHUNK
    key_iota = lax.broadcasted_iota(jnp.int32, (kt_rows, qb), 0)

    def score_tile(kt, carry):
        k0 = pl.multiple_of(kt * kt_rows, kt_rows)
        ki = kw_ref[pl.ds(k0, kt_rows), 0:IDX_DIM].astype(BF16)
        sc = jnp.zeros((kt_rows, qb), F32)
        for h in range(IDX_HEADS):
            dots = lax.dot_general(ki, qi_ref[:, h * IDX_DIM:(h + 1) * IDX_DIM], NT_DIMS,
                                   preferred_element_type=F32)
            sc = sc + w_t[IDX_DIM + h:IDX_DIM + h + 1, :] * jnp.maximum(dots, 0.0)
        sc = sc * ((IDX_DIM * IDX_HEADS) ** -0.5)
        bits = pltpu.bitcast(sc, jnp.int32)
        skey = bits ^ ((bits >> 31) & jnp.int32(0x7FFFFFFF))
        adm = (k0 + key_iota) // CHUNK <= q_chunk
        key_ref[pl.ds(k0, kt_rows), :] = jnp.where(adm, skey, jnp.int32(INT_MIN))
        return carry

    lax.fori_loop(0, n_kt, score_tile, 0)

    top_k = float(min(TOPK_MAX, seq // 4))

    def bit_step(it, v):
        cand = v + lax.shift_left(jnp.int32(1), 31 - it)

        def count_tile(kt, cnt):
            k0 = pl.multiple_of(kt * kt_rows, kt_rows)
            hit = jnp.where(key_ref[pl.ds(k0, kt_rows), :] >= cand, 1.0, 0.0)
            return cnt + jnp.sum(hit, axis=0, keepdims=True)

        cnt = lax.fori_loop(0, n_kt, count_tile, jnp.zeros((1, qb), F32))
        return jnp.where(cnt >= top_k, cand, v)

    v = lax.fori_loop(0, 32, bit_step, jnp.full((1, qb), INT_MIN, jnp.int32))
    v = jnp.maximum(v, jnp.int32(INT_MIN + 1))

    att_scale = HEAD_DIM ** -0.5
    m_ref[...] = jnp.full(m_ref.shape, NEG_BIG, F32)
    l_ref[...] = jnp.zeros(l_ref.shape, F32)
    acc_ref[...] = jnp.zeros(acc_ref.shape, F32)

    def att_tile(kt, carry):
        k0 = pl.multiple_of(kt * kt_rows, kt_rows)
        bias = jnp.where(key_ref[pl.ds(k0, kt_rows), :] >= v, 0.0, NEG_BIG)
        for h in range(A_HEADS):
            cols = slice(h * HEAD_DIM, (h + 1) * HEAD_DIM)
            st = lax.dot_general(ka_ref[pl.ds(k0, kt_rows), cols], qa_ref[:, cols], NT_DIMS,
                                 preferred_element_type=F32) * att_scale + bias
            m_old = m_ref[h:h + 1, :]
            m_new = jnp.maximum(m_old, jnp.max(st, axis=0, keepdims=True))
            alpha = jnp.exp(m_old - m_new)
            p = jnp.exp(st - m_new)
            m_ref[h:h + 1, :] = m_new
            l_ref[h:h + 1, :] = l_ref[h:h + 1, :] * alpha + jnp.sum(p, axis=0, keepdims=True)
            acc_ref[cols, :] = acc_ref[cols, :] * alpha + jnp.dot(
                vt_ref[cols, pl.ds(k0, kt_rows)], p.astype(BF16), preferred_element_type=F32)
        return carry

    lax.fori_loop(0, n_kt, att_tile, 0)
    for h in range(A_HEADS):
        cols = slice(h * HEAD_DIM, (h + 1) * HEAD_DIM)
        o_ref[:, cols] = (acc_ref[cols, :] / l_ref[h:h + 1, :]).T.astype(o_ref.dtype)


def _dsa(p1, p2, bsz, seq, col):
    n_blk = seq // DSA_QB
    width = A_HEADS * HEAD_DIM
    return pl.pallas_call(
        _dsa_kernel,
        out_shape=jax.ShapeDtypeStruct((bsz * seq, width), BF16),
        grid=(bsz, n_blk),
        in_specs=[pl.BlockSpec((DSA_QB, width), lambda b, k: (b * n_blk + k, col["qi"])),
                  pl.BlockSpec((seq, 128), lambda b, k: (b, col["kw"])),
                  pl.BlockSpec((DSA_QB, width), lambda b, k: (b * n_blk + k, col["qa"])),
                  pl.BlockSpec((seq, width), lambda b, k: (b, col["ka"])),
                  pl.BlockSpec((seq, width), lambda b, k: (b, col["va"]))],
        out_specs=pl.BlockSpec((DSA_QB, width), lambda b, k: (b * n_blk + k, 0)),
        scratch_shapes=[pltpu.VMEM((seq, DSA_QB), jnp.int32), pltpu.VMEM((width, seq), BF16),
                        pltpu.VMEM((A_HEADS, DSA_QB), F32), pltpu.VMEM((A_HEADS, DSA_QB), F32),
                        pltpu.VMEM((width, DSA_QB), F32)],
        compiler_params=_params(("arbitrary", "arbitrary")),
        name="dsa",
    )(p1, p2, p1, p1, p1)


HGRN_GROUP = 4


def _hgrn_kernel(q_ref, f_ref, i_ref, g_ref, lb_ref, gain_ref, o_ref,
                 qe_ref, oi_ref, upd_ref, dec_ref):
    c = CHUNK
    n_c = q_ref.shape[0] // c
    d = q_ref.shape[1]
    lb = lb_ref[0]
    gain = gain_ref[...]
    row = lax.broadcasted_iota(jnp.int32, (c, c), 0)
    colm = lax.broadcasted_iota(jnp.int32, (c, c), 1)
    tri = jnp.where(colm <= row, 1.0, 0.0).astype(F32)
    sub_row = lax.broadcasted_iota(jnp.int32, (SUB, d), 0)

    def local_part(ci):
        r0 = pl.multiple_of(ci * c, c)
        f = lb + (1.0 - lb) * _sigmoid(f_ref[pl.ds(r0, c), :])
        logf = jnp.log(f)
        k = 1.0 - f
        qr = q_ref[pl.ds(r0, c), :].astype(F32)
        q = qr * _sigmoid(qr)
        v = i_ref[pl.ds(r0, c), :].astype(F32)
        v16 = v.astype(BF16)
        b = jnp.dot(tri, logf, preferred_element_type=F32, precision=lax.Precision.HIGHEST)
        b_last = b[c - 1:c, :]
        qe_ref[pl.ds(r0, c), :] = (q * jnp.exp(b)).astype(BF16)

        for blk in range(c // SUB):
            lo = blk * SUB
            q_i, k_i, v_i, b_i = q[lo:lo + SUB], k[lo:lo + SUB], v[lo:lo + SUB], b[lo:lo + SUB]
            o_i = jnp.zeros((SUB, d), F32)
            if blk > 0:
                b_ref = b[lo - 1:lo, :]
                qt = (q_i * jnp.exp(b_i - b_ref)).astype(BF16)
                kt = (k[:lo] * jnp.exp(b_ref - b[:lo])).astype(BF16)
                att = lax.dot_general(qt, kt, NT_DIMS, preferred_element_type=F32)
                o_i = jnp.dot(att.astype(BF16), v16[:lo], preferred_element_type=F32)
            for s_ in range(SUB):
                w = jnp.exp(jnp.where(sub_row >= s_, b_i - b_i[s_:s_ + 1, :], -jnp.inf))
                a = jnp.sum(q_i * k_i[s_:s_ + 1, :] * w, axis=1, keepdims=True)
                o_i = o_i + a * v_i[s_:s_ + 1, :]
            oi_ref[pl.ds(r0 + lo, SUB), :] = o_i

        k_end = (k * jnp.exp(b_last - b)).astype(BF16)
        upd_ref[ci] = lax.dot_general(v16, k_end, TN_DIMS, preferred_element_type=F32)
        dec_ref[ci] = jnp.exp(b_last)

    def group(gi, carry):
        for u in range(HGRN_GROUP):
            local_part(gi * HGRN_GROUP + u)
        return carry

    lax.fori_loop(0, n_c // HGRN_GROUP, group, 0)

    st_t = jnp.zeros((d, d), F32)
    for ci in range(n_c):
        rows = slice(ci * c, (ci + 1) * c)
        o = oi_ref[rows, :] + lax.dot_general(qe_ref[rows, :], st_t.astype(BF16), NT_DIMS,
                                              preferred_element_type=F32)
        ms = jnp.mean(o * o, axis=-1, keepdims=True)
        rec = o * lax.rsqrt(ms + NORM_EPS) * gain
        gr = g_ref[rows, :].astype(F32)
        o_ref[rows, :] = (rec * (gr * _sigmoid(gr))).astype(o_ref.dtype)
        st_t = st_t * dec_ref[ci] + upd_ref[ci]


def _hgrn(p1, p2, lb, gain, bsz, seq, col):
    d = HEAD_DIM
    n_c = seq // CHUNK
    return pl.pallas_call(
        _hgrn_kernel,
        out_shape=jax.ShapeDtypeStruct((bsz * seq, B_HEADS * d), BF16),
        grid=(bsz, B_HEADS),
        in_specs=[pl.BlockSpec((seq, d), lambda b, h: (b, col["qb"] + h)),
                  pl.BlockSpec((seq, d), lambda b, h: (b, col["fb"] + h)),
                  pl.BlockSpec((seq, d), lambda b, h: (b, col["ib"] + h)),
                  pl.BlockSpec((seq, d), lambda b, h: (b, col["gb"] + h)),
                  pl.BlockSpec((1, 1, d), lambda b, h: (h, 0, 0)),
                  pl.BlockSpec((1, d), lambda b, h: (0, 0))],
        out_specs=pl.BlockSpec((seq, d), lambda b, h: (b, h)),
        scratch_shapes=[pltpu.VMEM((seq, d), BF16), pltpu.VMEM((seq, d), F32),
                        pltpu.VMEM((n_c, d, d), F32), pltpu.VMEM((n_c, 1, d), F32)],
        compiler_params=_params(("arbitrary", "arbitrary")),
        name="hgrn",
    )(p1, p2, p1, p1, lb.reshape(B_HEADS, 1, d), gain.reshape(1, d))


def _mix_kernel(attn_ref, rec_ref, ga_ref, gb_ref, x_ref, gate_ref, wa_ref, wb_ref, wo_ref,
                g2_ref, sc_ref, sh_ref, wq_ref, x1_ref, h2_ref, qp_ref):
    ya = jnp.dot(attn_ref[...], wa_ref[...], preferred_element_type=F32)
    yb = jnp.dot(rec_ref[...], wb_ref[...], preferred_element_type=F32)
    mixed = _sigmoid(ga_ref[...].astype(F32)) * ya + _sigmoid(gb_ref[...].astype(F32)) * yb
    x1 = x_ref[...] + gate_ref[0] * jnp.dot(mixed.astype(BF16), wo_ref[...],
                                            preferred_element_type=F32)
    x1_ref[...] = x1
    h2 = _norm_mod(x1, g2_ref[...], sc_ref[0], sh_ref[0]).astype(BF16)
    h2_ref[...] = h2
    qp_ref[...] = jnp.dot(h2, wq_ref[...], preferred_element_type=F32).astype(qp_ref.dtype)


def _mix(attn, rec, p1, x2d, gate1, wa, wb, wo, g2, scale2, shift2, wq, seq, tm, col):
    t, d = x2d.shape
    wa_w = attn.shape[1]
    per_b = seq // tm
    full = lambda shape: pl.BlockSpec(shape, lambda i: (0,) * len(shape),
                                      pipeline_mode=pl.Buffered(1))
    mod = pl.BlockSpec((1, 1, d), lambda i: (i // per_b, 0, 0))
    return pl.pallas_call(
        _mix_kernel,
        out_shape=(jax.ShapeDtypeStruct((t, d), F32), jax.ShapeDtypeStruct((t, d), BF16),
                   jax.ShapeDtypeStruct((t, wq.shape[1]), BF16)),
        grid=(t // tm,),
        in_specs=[pl.BlockSpec((tm, wa_w), lambda i: (i, 0)),
                  pl.BlockSpec((tm, wa_w), lambda i: (i, 0)),
                  pl.BlockSpec((tm, d), lambda i: (i, col["g_a"])),
                  pl.BlockSpec((tm, d), lambda i: (i, col["g_b"])),
                  pl.BlockSpec((tm, d), lambda i: (i, 0)),
                  mod, full(wa.shape), full(wb.shape), full(wo.shape),
                  full((1, d)), mod, mod, full(wq.shape)],
        out_specs=(pl.BlockSpec((tm, d), lambda i: (i, 0)),
                   pl.BlockSpec((tm, d), lambda i: (i, 0)),
                   pl.BlockSpec((tm, wq.shape[1]), lambda i: (i, 0))),
        compiler_params=_params(("arbitrary",)),
        name="mix",
    )(attn, rec, p1, p1, x2d, gate1, wa, wb, wo, g2.reshape(1, d), scale2, shift2, wq)


def _top_values(work, n, dst_ref):
    rank = jnp.full(work.shape, float(n), F32)
    for r in range(n):
        mx = jnp.max(work, axis=0, keepdims=True)
        dst_ref[r:r + 1, :] = mx
        hit = work == mx
        rank = jnp.where(hit, float(r), rank)
        work = jnp.where(hit, -jnp.inf, work)
    return rank


def _gelu(x):
    return 0.5 * x * (1.0 + lax.erf(x * (2.0 ** -0.5)))


PEER_LANES = 128


def _peer_stats(qp_ref, keys_ref, m_ref, rank_ref, c0_ref, e1_ref, top_ref, cand_ref, best_ref):
    nk, kk, tl = PEER_N_KEYS, PEER_TOPK, PEER_LANES
    row8 = lax.broadcasted_iota(jnp.int32, (8, tl), 0)

    def chunk(c, carry):
        t0 = pl.multiple_of(c * tl, tl)
        for p in range(PEER_HEADS):
            scores = []
            for half in range(2):
                j = 2 * p + half
                qs = qp_ref[pl.ds(t0, tl), j * nk:(j + 1) * nk]
                scores.append(lax.dot_general(keys_ref[j], qs, NT_DIMS,
                                              preferred_element_type=F32))
            s0, s1 = scores
            _top_values(s0, kk, top_ref.at[0])
            rank_ref[c, p] = _top_values(s1, kk, top_ref.at[1]).astype(BF16)
            cand_ref[0:kk, :] = top_ref[0, 0:1, :] + top_ref[1]
            cand_ref[kk:kk + 8, :] = top_ref[0, 1:2, :] + top_ref[1, 0:8, :]
            for a in range(2, 8):
                grp = top_ref[0, a:a + 1, :] + top_ref[1, 0:8, :]
                cand_ref[8 + 8 * a:16 + 8 * a, :] = jnp.where(row8 < kk // (a + 1), grp, -jnp.inf)
            cand_ref[72:80, :] = top_ref[0, 8:16, :] + top_ref[1, 0:1, :]
            _top_values(cand_ref[...], kk, best_ref)
            best = best_ref[...]
            z = jnp.sum(jnp.exp(best - best[0:1, :]), axis=0, keepdims=True)
            thr = best[kk - 1:kk, :]
            m = jnp.zeros(s0.shape, F32)
            for b in range(kk):
                m = m + jnp.where(s0 + top_ref[1, b:b + 1, :] >= thr, 1.0, 0.0)
            m_ref[c, p] = m
            c0_ref[c, p] = jnp.exp(s0 - top_ref[0, 0:1, :]) / z
            e1_ref[c, p] = jnp.exp(s1 - top_ref[1, 0:1, :]).astype(BF16)
        return carry

    lax.fori_loop(0, m_ref.shape[0], chunk, 0)


def _peer_kernel(qp_ref, keys_ref, h2_ref, u_ref, vt_ref, x1_ref, gate_ref, fin_ref, o_ref,
                 m_ref, rank_ref, c0_ref, e1_ref, acc_ref, top_ref, cand_ref, best_ref, act_ref):
    eb = pl.program_id(1)
    n_eb = pl.num_programs(1)
    nk, tl = PEER_N_KEYS, PEER_LANES
    rows = u_ref.shape[0] // nk

    @pl.when(eb == 0)
    def _():
        acc_ref[...] = jnp.zeros_like(acc_ref)
        _peer_stats(qp_ref, keys_ref, m_ref, rank_ref, c0_ref, e1_ref, top_ref, cand_ref, best_ref)

    a_t = lax.dot_general(u_ref[...], h2_ref[...], NT_DIMS, preferred_element_type=F32)
    for r in range(rows):
        i = eb * rows + r
        sl = slice(r * nk, (r + 1) * nk)
        for c in range(m_ref.shape[0]):
            lanes = slice(c * tl, (c + 1) * tl)
            gmat = jnp.zeros((nk, tl), BF16)
            for p in range(PEER_HEADS):
                c0row = jnp.broadcast_to(c0_ref[c, p, pl.ds(i, 1), :].astype(BF16), (nk, tl))
                mrow = jnp.broadcast_to(m_ref[c, p, pl.ds(i, 1), :].astype(BF16), (nk, tl))
                val = c0row * e1_ref[c, p]
                gmat = gmat + jnp.where(rank_ref[c, p] < mrow, val, jnp.zeros_like(val))
            act_ref[sl, lanes] = _gelu(a_t[sl, lanes]).astype(BF16) * gmat
    acc_ref[...] += jnp.dot(vt_ref[...], act_ref[...], preferred_element_type=F32)

    @pl.when(eb == n_eb - 1)
    def _():
        x2 = x1_ref[...] + gate_ref[0] * acc_ref[...].T
        ms = jnp.mean(x2 * x2, axis=-1, keepdims=True)
        o_ref[...] = x2 * lax.rsqrt(ms + NORM_EPS) * fin_ref[...]


def _peer(qp, keys, h2, u, vt, x1, gate2, fin, seq, tm, te):
    t, d = x1.shape
    per_b = seq // tm
    nk, tl = PEER_N_KEYS, PEER_LANES
    stat = pltpu.VMEM((tm // tl, PEER_HEADS, nk, tl), F32)
    stat16 = pltpu.VMEM((tm // tl, PEER_HEADS, nk, tl), BF16)
    return pl.pallas_call(
        _peer_kernel,
        out_shape=jax.ShapeDtypeStruct((t, d), F32),
        grid=(t // tm, u.shape[0] // te),
        in_specs=[pl.BlockSpec((tm, qp.shape[1]), lambda i, e: (i, 0)),
                  pl.BlockSpec(keys.shape, lambda i, e: (0, 0, 0)),
                  pl.BlockSpec((tm, d), lambda i, e: (i, 0)),
                  pl.BlockSpec((te, d), lambda i, e: (e, 0)),
                  pl.BlockSpec((d, te), lambda i, e: (0, e)),
                  pl.BlockSpec((tm, d), lambda i, e: (i, 0)),
                  pl.BlockSpec((1, 1, d), lambda i, e: (i // per_b, 0, 0)),
                  pl.BlockSpec((1, d), lambda i, e: (0, 0))],
        out_specs=pl.BlockSpec((tm, d), lambda i, e: (i, 0)),
        scratch_shapes=[stat, stat16, stat, stat16,
                        pltpu.VMEM((d, tm), F32),
                        pltpu.VMEM((2, PEER_TOPK, tl), F32),
                        pltpu.VMEM((80, tl), F32),
                        pltpu.VMEM((PEER_TOPK, tl), F32),
                        pltpu.VMEM((te, tm), BF16)],
        compiler_params=_params(("arbitrary", "arbitrary")),
        name="peer",
    )(qp, keys, h2, u, vt, x1, gate2, fin.reshape(1, d))


def kernel(x, c, w_ada, b_ada, norm_mix, norm_ffn, w_in, lb_logits, hgrn_gain, w_up_a, w_up_b,
           w_out, peer_w_q, peer_keys, peer_u, peer_v, final_norm):
    bsz, seq, d = x.shape
    depth = w_ada.shape[0]
    assert depth == 1, "the PEER kernel applies the final norm, so it must be the last layer"
    aw = A_HEADS * HEAD_DIM
    bw = B_HEADS * HEAD_DIM
    iw = IDX_HEADS * IDX_DIM
    lower_bounds = jnp.cumsum(jax.nn.softmax(lb_logits.astype(F32), axis=0), axis=0)
    xt = x.reshape(bsz * seq, d)

    widths = (aw, aw, aw, iw, IDX_DIM, IDX_HEADS, bw, bw, bw, bw, d, d)
    names = ("qa", "ka", "va", "qi", "ki", "wi", "qb", "fb", "ib", "gb", "g_a", "g_b")
    off, start = {}, 0
    for nme, wd in zip(names, widths):
        off[nme] = (start, start + wd)
        start += wd
    order1 = ("g_a", "g_b", "qa", "ka", "va", "qi", "qb", "ib", "gb")
    order2 = ("fb", "ki", "wi")
    pad2 = 128 - IDX_DIM - IDX_HEADS

    for l in range(depth):
        mod = _ada(c, w_ada[l], b_ada[l])
        shift1, scale1, gate1, shift2, scale2, gate2 = [
            m.reshape(bsz, 1, d) for m in jnp.split(mod, 6, axis=-1)]

        wl = w_in[l]
        w1 = jnp.concatenate([wl[:, off[n][0]:off[n][1]] for n in order1], axis=1).astype(BF16)
        w2 = jnp.concatenate([wl[:, off[n][0]:off[n][1]] for n in order2]
                             + [jnp.zeros((d, pad2), wl.dtype)], axis=1).astype(BF16)
        p1 = _inproj(xt, norm_mix[l], scale1, shift1, w1, BF16, seq, tm=1024, tn=1024)
        p2 = _inproj(xt, norm_mix[l], scale1, shift1, w2, F32, seq, tm=1024, tn=w2.shape[1])

        attn = _dsa(p1, p2, bsz, seq, {"qa": 4, "ka": 5, "va": 6, "qi": 7, "kw": 8})
        rec = _hgrn(p1, p2, lower_bounds[l], hgrn_gain[l], bsz, seq,
                    {"qb": 64, "ib": 72, "gb": 80, "fb": 0})

        x1, h2, qp = _mix(attn, rec, p1, xt, gate1,
                          w_up_a[l].astype(BF16), w_up_b[l].astype(BF16), w_out[l].astype(BF16),
                          norm_ffn[l], scale2, shift2, peer_w_q[l].astype(BF16),
                          seq, tm=256, col={"g_a": 0, "g_b": 1})
        keys = peer_keys[l].reshape(2 * PEER_HEADS, PEER_N_KEYS, -1).astype(BF16)
        xt = _peer(qp, keys, h2, peer_u[l].astype(BF16), peer_v[l].T.astype(BF16), x1, gate2,
                   final_norm, seq, tm=512, te=512)
    return xt.reshape(bsz, seq, d)
```

```python
import functools

import jax
import jax.numpy as jnp
from jax import lax
from jax.experimental import pallas as pl
from jax.experimental.pallas import tpu as pltpu

F32 = jnp.float32
BF16 = jnp.bfloat16

NORM_EPS = 1e-6
CHUNK = 64
TOPK_MAX = 256
A_HEADS = 8
HEAD_DIM = 128
IDX_HEADS = 16
IDX_DIM = 64
B_HEADS = 8
PEER_HEADS = 8
PEER_N_KEYS = 128
PEER_TOPK = 16
SUB = 16

VMEM_LIMIT = 56 * 1024 * 1024
INT_MIN = -2 ** 31

NT_DIMS = (((1,), (1,)), ((), ()))
TN_DIMS = (((0,), (0,)), ((), ()))


def _sigmoid(x):
    return 1.0 / (1.0 + jnp.exp(-x))


def _params(sem):
    return pltpu.CompilerParams(dimension_semantics=sem, vmem_limit_bytes=VMEM_LIMIT)


def _ada_kernel(c_ref, w_ref, b_ref, o_ref):
    c = c_ref[...]
    cond = (c * _sigmoid(c)).astype(BF16)
    o_ref[...] = jnp.dot(cond, w_ref[...].astype(BF16), preferred_element_type=F32) + b_ref[...]


def _ada(c, w, b, tn=1024):
    bsz, d = c.shape
    n = w.shape[1]
    return pl.pallas_call(
        _ada_kernel,
        out_shape=jax.ShapeDtypeStruct((bsz, n), F32),
        grid=(n // tn,),
        in_specs=[pl.BlockSpec((bsz, d), lambda j: (0, 0)),
                  pl.BlockSpec((d, tn), lambda j: (0, j)),
                  pl.BlockSpec((1, tn), lambda j: (0, j))],
        out_specs=pl.BlockSpec((bsz, tn), lambda j: (0, j)),
        compiler_params=_params(("arbitrary",)),
        name="ada",
    )(c, w, b.reshape(1, n))


def _norm_mod(x, gain, scale, shift):
    ms = jnp.mean(x * x, axis=-1, keepdims=True)
    y = x * lax.rsqrt(ms + NORM_EPS) * gain
    return y * (1.0 + scale) + shift


def _inproj_kernel(x_ref, g_ref, sc_ref, sh_ref, w_ref, o_ref, h_ref):
    @pl.when(pl.program_id(1) == 0)
    def _():
        h_ref[...] = _norm_mod(x_ref[...], g_ref[...], sc_ref[0], sh_ref[0]).astype(BF16)

    o_ref[...] = jnp.dot(h_ref[...], w_ref[...], preferred_element_type=F32).astype(o_ref.dtype)


def _inproj(x2d, gain, scale, shift, w, out_dtype, seq, tm, tn):
    t, d = x2d.shape
    n = w.shape[1]
    per_b = seq // tm
    return pl.pallas_call(
        _inproj_kernel,
        out_shape=jax.ShapeDtypeStruct((t, n), out_dtype),
        grid=(t // tm, n // tn),
        in_specs=[pl.BlockSpec((tm, d), lambda i, j: (i, 0)),
                  pl.BlockSpec((1, d), lambda i, j: (0, 0)),
                  pl.BlockSpec((1, 1, d), lambda i, j: (i // per_b, 0, 0)),
                  pl.BlockSpec((1, 1, d), lambda i, j: (i // per_b, 0, 0)),
                  pl.BlockSpec((d, tn), lambda i, j: (0, j))],
        out_specs=pl.BlockSpec((tm, tn), lambda i, j: (i, j)),
        scratch_shapes=[pltpu.VMEM((tm, d), BF16)],
        compiler_params=_params(("arbitrary", "arbitrary")),
        name="inproj",
    )(x2d, gain.reshape(1, d), scale, shift, w)


DSA_QB = 256
DSA_KT = 256
DSA_AT = 128
NEG_BIG = -1e30


def _dsa_kernel(qi_ref, kw_ref, qa_ref, ka_ref, va_ref, o_ref,
                key_ref, vt_ref, m_ref, l_ref, acc_ref):
    j = pl.program_id(1)
    qb, kt_rows = DSA_QB, DSA_KT
    seq = ka_ref.shape[0]
    n_kt = (j * qb + qb + kt_rows - 1) // kt_rows
    r0 = pl.multiple_of(j * qb, qb)

    @pl.when(j == 0)
    def _():
        for h in range(A_HEADS):
            for t in range(seq // kt_rows):
                blk = va_ref[t * kt_rows:(t + 1) * kt_rows, h * HEAD_DIM:(h + 1) * HEAD_DIM]
                vt_ref[h * HEAD_DIM:(h + 1) * HEAD_DIM, t * kt_rows:(t + 1) * kt_rows] = (
                    blk.astype(F32).T.astype(BF16))

    w_t = kw_ref[pl.ds(r0, qb), :].T
    q_chunk = (r0 + lax.broadcasted_iota(jnp.int32, (kt_rows, qb), 1)) // CHUNK
    key_iota = lax.broadcasted_iota(jnp.int32, (kt_rows, qb), 0)

    def score_tile(kt, carry):
        k0 = pl.multiple_of(kt * kt_rows, kt_rows)
        ki = kw_ref[pl.ds(k0, kt_rows), 0:IDX_DIM].astype(BF16)
        sc = jnp.zeros((kt_rows, qb), F32)
        for h in range(IDX_HEADS):
            dots = lax.dot_general(ki, qi_ref[:, h * IDX_DIM:(h + 1) * IDX_DIM], NT_DIMS,
                                   preferred_element_type=F32)
            sc = sc + w_t[IDX_DIM + h:IDX_DIM + h + 1, :] * jnp.maximum(dots, 0.0)
        sc = sc * ((IDX_DIM * IDX_HEADS) ** -0.5)
        bits = pltpu.bitcast(sc, jnp.int32)
        skey = bits ^ ((bits >> 31) & jnp.int32(0x7FFFFFFF))
        adm = (k0 + key_iota) // CHUNK <= q_chunk
        key_ref[pl.ds(k0, kt_rows), :] = jnp.where(adm, skey, jnp.int32(INT_MIN))
        return carry

    lax.fori_loop(0, n_kt, score_tile, 0)

    top_k = float(min(TOPK_MAX, seq // 4))

    def bit_step(it, v):
        cand = v + lax.shift_left(jnp.int32(1), 31 - it)

        def count_tile(kt, cnt):
            k0 = pl.multiple_of(kt * kt_rows, kt_rows)
            hit = jnp.where(key_ref[pl.ds(k0, kt_rows), :] >= cand, 1.0, 0.0)
            return cnt + jnp.sum(hit, axis=0, keepdims=True)

        cnt = lax.fori_loop(0, n_kt, count_tile, jnp.zeros((1, qb), F32))
        return jnp.where(cnt >= top_k, cand, v)

    v = lax.fori_loop(0, 32, bit_step, jnp.full((1, qb), INT_MIN, jnp.int32))
    v = jnp.maximum(v, jnp.int32(INT_MIN + 1))

    att_scale = HEAD_DIM ** -0.5
    m_ref[...] = jnp.full(m_ref.shape, NEG_BIG, F32)
    l_ref[...] = jnp.zeros(l_ref.shape, F32)
    acc_ref[...] = jnp.zeros(acc_ref.shape, F32)

    at_rows = DSA_AT

    def att_tile(kt, carry):
        k0 = pl.multiple_of(kt * at_rows, at_rows)
        bias = jnp.where(key_ref[pl.ds(k0, at_rows), :] >= v, 0.0, NEG_BIG)
        for h in range(A_HEADS):
            cols = slice(h * HEAD_DIM, (h + 1) * HEAD_DIM)
            st = lax.dot_general(ka_ref[pl.ds(k0, at_rows), cols], qa_ref[:, cols], NT_DIMS,
                                 preferred_element_type=F32) * att_scale + bias
            m_old = m_ref[h:h + 1, :]
            m_new = jnp.maximum(m_old, jnp.max(st, axis=0, keepdims=True))
            alpha = jnp.exp(m_old - m_new)
            p = jnp.exp(st - m_new)
            m_ref[h:h + 1, :] = m_new
            l_ref[h:h + 1, :] = l_ref[h:h + 1, :] * alpha + jnp.sum(p, axis=0, keepdims=True)
            acc_ref[cols, :] = acc_ref[cols, :] * alpha + jnp.dot(
                vt_ref[cols, pl.ds(k0, at_rows)], p.astype(BF16), preferred_element_type=F32)
        return carry

    lax.fori_loop(0, n_kt * (kt_rows // at_rows), att_tile, 0)
    for h in range(A_HEADS):
        cols = slice(h * HEAD_DIM, (h + 1) * HEAD_DIM)
        o_ref[:, cols] = (acc_ref[cols, :] / l_ref[h:h + 1, :]).T.astype(o_ref.dtype)


def _dsa(p1, p2, bsz, seq, col):
    n_blk = seq // DSA_QB
    width = A_HEADS * HEAD_DIM
    return pl.pallas_call(
        _dsa_kernel,
        out_shape=jax.ShapeDtypeStruct((bsz * seq, width), BF16),
        grid=(bsz, n_blk),
        in_specs=[pl.BlockSpec((DSA_QB, width), lambda b, k: (b * n_blk + k, col["qi"])),
                  pl.BlockSpec((seq, 128), lambda b, k: (b, col["kw"])),
                  pl.BlockSpec((DSA_QB, width), lambda b, k: (b * n_blk + k, col["qa"])),
                  pl.BlockSpec((seq, width), lambda b, k: (b, col["ka"])),
                  pl.BlockSpec((seq, width), lambda b, k: (b, col["va"]))],
        out_specs=pl.BlockSpec((DSA_QB, width), lambda b, k: (b * n_blk + k, 0)),
        scratch_shapes=[pltpu.VMEM((seq, DSA_QB), jnp.int32), pltpu.VMEM((width, seq), BF16),
                        pltpu.VMEM((A_HEADS, DSA_QB), F32), pltpu.VMEM((A_HEADS, DSA_QB), F32),
                        pltpu.VMEM((width, DSA_QB), F32)],
        compiler_params=_params(("arbitrary", "arbitrary")),
        name="dsa",
    )(p1, p2, p1, p1, p1)


HGRN_GROUP = 4


def _hgrn_kernel(q_ref, f_ref, i_ref, g_ref, lb_ref, gain_ref, o_ref,
                 qe_ref, oi_ref, upd_ref, dec_ref):
    c = CHUNK
    n_c = q_ref.shape[0] // c
    d = q_ref.shape[1]
    lb = lb_ref[0]
    gain = gain_ref[...]
    chunk_row = lax.broadcasted_iota(jnp.int32, (c, d), 0)
    sub_row = lax.broadcasted_iota(jnp.int32, (SUB, d), 0)

    def local_part(ci):
        r0 = pl.multiple_of(ci * c, c)
        f = lb + (1.0 - lb) * _sigmoid(f_ref[pl.ds(r0, c), :])
        logf = jnp.log(f)
        k = 1.0 - f
        qr = q_ref[pl.ds(r0, c), :].astype(F32)
        q = qr * _sigmoid(qr)
        v = i_ref[pl.ds(r0, c), :].astype(F32)
        v16 = v.astype(BF16)
        b = logf
        shift = 1
        while shift < c:
            b = b + jnp.where(chunk_row >= shift, pltpu.roll(b, shift, 0), 0.0)
            shift *= 2
        b_last = b[c - 1:c, :]
        qe_ref[pl.ds(r0, c), :] = (q * jnp.exp(b)).astype(BF16)

        for blk in range(c // SUB):
            lo = blk * SUB
            q_i, k_i, v_i, b_i = q[lo:lo + SUB], k[lo:lo + SUB], v[lo:lo + SUB], b[lo:lo + SUB]
            o_i = jnp.zeros((SUB, d), F32)
            if blk > 0:
                b_ref = b[lo - 1:lo, :]
                qt = (q_i * jnp.exp(b_i - b_ref)).astype(BF16)
                kt = (k[:lo] * jnp.exp(b_ref - b[:lo])).astype(BF16)
                att = lax.dot_general(qt, kt, NT_DIMS, preferred_element_type=F32)
                o_i = jnp.dot(att.astype(BF16), v16[:lo], preferred_element_type=F32)
            for s_ in range(SUB):
                w = jnp.exp(jnp.where(sub_row >= s_, b_i - b_i[s_:s_ + 1, :], -jnp.inf))
                a = jnp.sum(q_i * k_i[s_:s_ + 1, :] * w, axis=1, keepdims=True)
                o_i = o_i + a * v_i[s_:s_ + 1, :]
            oi_ref[pl.ds(r0 + lo, SUB), :] = o_i

        k_end = (k * jnp.exp(b_last - b)).astype(BF16)
        upd_ref[ci] = lax.dot_general(v16, k_end, TN_DIMS, preferred_element_type=F32)
        dec_ref[ci] = jnp.exp(b_last)

    def group(gi, carry):
        for u in range(HGRN_GROUP):
            local_part(gi * HGRN_GROUP + u)
        return carry

    lax.fori_loop(0, n_c // HGRN_GROUP, group, 0)

    st_t = jnp.zeros((d, d), F32)
    for ci in range(n_c):
        rows = slice(ci * c, (ci + 1) * c)
        o = oi_ref[rows, :] + lax.dot_general(qe_ref[rows, :], st_t.astype(BF16), NT_DIMS,
                                              preferred_element_type=F32)
        ms = jnp.mean(o * o, axis=-1, keepdims=True)
        rec = o * lax.rsqrt(ms + NORM_EPS) * gain
        gr = g_ref[rows, :].astype(F32)
        o_ref[rows, :] = (rec * (gr * _sigmoid(gr))).astype(o_ref.dtype)
        st_t = st_t * dec_ref[ci] + upd_ref[ci]


def _hgrn(p1, p2, lb, gain, bsz, seq, col):
    d = HEAD_DIM
    n_c = seq // CHUNK
    return pl.pallas_call(
        _hgrn_kernel,
        out_shape=jax.ShapeDtypeStruct((bsz * seq, B_HEADS * d), BF16),
        grid=(bsz, B_HEADS),
        in_specs=[pl.BlockSpec((seq, d), lambda b, h: (b, col["qb"] + h)),
                  pl.BlockSpec((seq, d), lambda b, h: (b, col["fb"] + h)),
                  pl.BlockSpec((seq, d), lambda b, h: (b, col["ib"] + h)),
                  pl.BlockSpec((seq, d), lambda b, h: (b, col["gb"] + h)),
                  pl.BlockSpec((1, 1, d), lambda b, h: (h, 0, 0)),
                  pl.BlockSpec((1, d), lambda b, h: (0, 0))],
        out_specs=pl.BlockSpec((seq, d), lambda b, h: (b, h)),
        scratch_shapes=[pltpu.VMEM((seq, d), BF16), pltpu.VMEM((seq, d), F32),
                        pltpu.VMEM((n_c, d, d), F32), pltpu.VMEM((n_c, 1, d), F32)],
        compiler_params=_params(("arbitrary", "arbitrary")),
        name="hgrn",
    )(p1, p2, p1, p1, lb.reshape(B_HEADS, 1, d), gain.reshape(1, d))


def _mix_kernel(attn_ref, rec_ref, ga_ref, gb_ref, x_ref, gate_ref, wa_ref, wb_ref, wo_ref,
                g2_ref, sc_ref, sh_ref, wq_ref, x1_ref, h2_ref, qp_ref):
    ya = jnp.dot(attn_ref[...], wa_ref[...], preferred_element_type=F32)
    yb = jnp.dot(rec_ref[...], wb_ref[...], preferred_element_type=F32)
    mixed = _sigmoid(ga_ref[...].astype(F32)) * ya + _sigmoid(gb_ref[...].astype(F32)) * yb
    x1 = x_ref[...] + gate_ref[0] * jnp.dot(mixed.astype(BF16), wo_ref[...],
                                            preferred_element_type=F32)
    x1_ref[...] = x1
    h2 = _norm_mod(x1, g2_ref[...], sc_ref[0], sh_ref[0]).astype(BF16)
    h2_ref[...] = h2
    qp_ref[...] = jnp.dot(h2, wq_ref[...], preferred_element_type=F32).astype(qp_ref.dtype)


def _mix(attn, rec, p1, x2d, gate1, wa, wb, wo, g2, scale2, shift2, wq, seq, tm, col):
    t, d = x2d.shape
    wa_w = attn.shape[1]
    per_b = seq // tm
    full = lambda shape: pl.BlockSpec(shape, lambda i: (0,) * len(shape),
                                      pipeline_mode=pl.Buffered(1))
    mod = pl.BlockSpec((1, 1, d), lambda i: (i // per_b, 0, 0))
    return pl.pallas_call(
        _mix_kernel,
        out_shape=(jax.ShapeDtypeStruct((t, d), F32), jax.ShapeDtypeStruct((t, d), BF16),
                   jax.ShapeDtypeStruct((t, wq.shape[1]), BF16)),
        grid=(t // tm,),
        in_specs=[pl.BlockSpec((tm, wa_w), lambda i: (i, 0)),
                  pl.BlockSpec((tm, wa_w), lambda i: (i, 0)),
                  pl.BlockSpec((tm, d), lambda i: (i, col["g_a"])),
                  pl.BlockSpec((tm, d), lambda i: (i, col["g_b"])),
                  pl.BlockSpec((tm, d), lambda i: (i, 0)),
                  mod, full(wa.shape), full(wb.shape), full(wo.shape),
                  full((1, d)), mod, mod, full(wq.shape)],
        out_specs=(pl.BlockSpec((tm, d), lambda i: (i, 0)),
                   pl.BlockSpec((tm, d), lambda i: (i, 0)),
                   pl.BlockSpec((tm, wq.shape[1]), lambda i: (i, 0))),
        compiler_params=_params(("arbitrary",)),
        name="mix",
    )(attn, rec, p1, p1, x2d, gate1, wa, wb, wo, g2.reshape(1, d), scale2, shift2, wq)


def _top_values(work, n, dst_ref):
    rank = jnp.full(work.shape, float(n), F32)
    for r in range(n):
        mx = jnp.max(work, axis=0, keepdims=True)
        dst_ref[r:r + 1, :] = mx
        hit = work == mx
        rank = jnp.where(hit, float(r), rank)
        work = jnp.where(hit, -jnp.inf, work)
    return rank


def _gelu(x):
    return 0.5 * x * (1.0 + lax.erf(x * (2.0 ** -0.5)))


PEER_LANES = 128


def _peer_stats(qp_ref, keys_ref, m_ref, rank_ref, c0_ref, e1_ref, top_ref, cand_ref, best_ref):
    nk, kk, tl = PEER_N_KEYS, PEER_TOPK, PEER_LANES
    row8 = lax.broadcasted_iota(jnp.int32, (8, tl), 0)

    def chunk(c, carry):
        t0 = pl.multiple_of(c * tl, tl)
        for p in range(PEER_HEADS):
            scores = []
            for half in range(2):
                j = 2 * p + half
                qs = qp_ref[pl.ds(t0, tl), j * nk:(j + 1) * nk]
                scores.append(lax.dot_general(keys_ref[j], qs, NT_DIMS,
                                              preferred_element_type=F32))
            s0, s1 = scores
            _top_values(s0, kk, top_ref.at[0])
            rank_ref[c, p] = _top_values(s1, kk, top_ref.at[1]).astype(BF16)
            cand_ref[0:kk, :] = top_ref[0, 0:1, :] + top_ref[1]
            cand_ref[kk:kk + 8, :] = top_ref[0, 1:2, :] + top_ref[1, 0:8, :]
            for a in range(2, 8):
                grp = top_ref[0, a:a + 1, :] + top_ref[1, 0:8, :]
                cand_ref[8 + 8 * a:16 + 8 * a, :] = jnp.where(row8 < kk // (a + 1), grp, -jnp.inf)
            cand_ref[72:80, :] = top_ref[0, 8:16, :] + top_ref[1, 0:1, :]
            _top_values(cand_ref[...], kk, best_ref)
            best = best_ref[...]
            z = jnp.sum(jnp.exp(best - best[0:1, :]), axis=0, keepdims=True)
            thr = best[kk - 1:kk, :]
            m = jnp.zeros(s0.shape, F32)
            for b in range(kk):
                m = m + jnp.where(s0 + top_ref[1, b:b + 1, :] >= thr, 1.0, 0.0)
            m_ref[c, p] = m
            c0_ref[c, p] = jnp.exp(s0 - top_ref[0, 0:1, :]) / z
            e1_ref[c, p] = jnp.exp(s1 - top_ref[1, 0:1, :]).astype(BF16)
        return carry

    lax.fori_loop(0, m_ref.shape[0], chunk, 0)


def _peer_kernel(qp_ref, keys_ref, h2_ref, u_ref, vt_ref, x1_ref, gate_ref, fin_ref, o_ref,
                 m_ref, rank_ref, c0_ref, e1_ref, acc_ref, top_ref, cand_ref, best_ref, act_ref):
    eb = pl.program_id(1)
    n_eb = pl.num_programs(1)
    nk, tl = PEER_N_KEYS, PEER_LANES
    rows = u_ref.shape[0] // nk

    @pl.when(eb == 0)
    def _():
        acc_ref[...] = jnp.zeros_like(acc_ref)
        _peer_stats(qp_ref, keys_ref, m_ref, rank_ref, c0_ref, e1_ref, top_ref, cand_ref, best_ref)

    a_t = lax.dot_general(u_ref[...], h2_ref[...], NT_DIMS, preferred_element_type=F32)
    for r in range(rows):
        i = eb * rows + r
        sl = slice(r * nk, (r + 1) * nk)
        for c in range(m_ref.shape[0]):
            lanes = slice(c * tl, (c + 1) * tl)
            gmat = jnp.zeros((nk, tl), BF16)
            for p in range(PEER_HEADS):
                c0row = jnp.broadcast_to(c0_ref[c, p, pl.ds(i, 1), :].astype(BF16), (nk, tl))
                mrow = jnp.broadcast_to(m_ref[c, p, pl.ds(i, 1), :].astype(BF16), (nk, tl))
                val = c0row * e1_ref[c, p]
                gmat = gmat + jnp.where(rank_ref[c, p] < mrow, val, jnp.zeros_like(val))
            act_ref[sl, lanes] = _gelu(a_t[sl, lanes]).astype(BF16) * gmat
    acc_ref[...] += jnp.dot(vt_ref[...], act_ref[...], preferred_element_type=F32)

    @pl.when(eb == n_eb - 1)
    def _():
        x2 = x1_ref[...] + gate_ref[0] * acc_ref[...].T
        ms = jnp.mean(x2 * x2, axis=-1, keepdims=True)
        o_ref[...] = x2 * lax.rsqrt(ms + NORM_EPS) * fin_ref[...]


def _peer(qp, keys, h2, u, vt, x1, gate2, fin, seq, tm, te):
    t, d = x1.shape
    per_b = seq // tm
    nk, tl = PEER_N_KEYS, PEER_LANES
    stat = pltpu.VMEM((tm // tl, PEER_HEADS, nk, tl), F32)
    stat16 = pltpu.VMEM((tm // tl, PEER_HEADS, nk, tl), BF16)
    once = pl.Buffered(1)
    return pl.pallas_call(
        _peer_kernel,
        out_shape=jax.ShapeDtypeStruct((t, d), F32),
        grid=(t // tm, u.shape[0] // te),
        in_specs=[pl.BlockSpec((tm, qp.shape[1]), lambda i, e: (i, 0), pipeline_mode=once),
                  pl.BlockSpec(keys.shape, lambda i, e: (0, 0, 0), pipeline_mode=once),
                  pl.BlockSpec((tm, d), lambda i, e: (i, 0)),
                  pl.BlockSpec((te, d), lambda i, e: (e, 0)),
                  pl.BlockSpec((d, te), lambda i, e: (0, e)),
                  pl.BlockSpec((tm, d), lambda i, e: (i, 0), pipeline_mode=once),
                  pl.BlockSpec((1, 1, d), lambda i, e: (i // per_b, 0, 0)),
                  pl.BlockSpec((1, d), lambda i, e: (0, 0))],
        out_specs=pl.BlockSpec((tm, d), lambda i, e: (i, 0)),
        scratch_shapes=[stat, stat16, stat, stat16,
                        pltpu.VMEM((d, tm), F32),
                        pltpu.VMEM((2, PEER_TOPK, tl), F32),
                        pltpu.VMEM((80, tl), F32),
                        pltpu.VMEM((PEER_TOPK, tl), F32),
                        pltpu.VMEM((te, tm), BF16)],
        compiler_params=_params(("arbitrary", "arbitrary")),
        name="peer",
    )(qp, keys, h2, u, vt, x1, gate2, fin.reshape(1, d))


def kernel(x, c, w_ada, b_ada, norm_mix, norm_ffn, w_in, lb_logits, hgrn_gain, w_up_a, w_up_b,
           w_out, peer_w_q, peer_keys, peer_u, peer_v, final_norm):
    bsz, seq, d = x.shape
    depth = w_ada.shape[0]
    assert depth == 1, "the PEER kernel applies the final norm, so it must be the last layer"
    aw = A_HEADS * HEAD_DIM
    bw = B_HEADS * HEAD_DIM
    iw = IDX_HEADS * IDX_DIM
    lower_bounds = jnp.cumsum(jax.nn.softmax(lb_logits.astype(F32), axis=0), axis=0)
    xt = x.reshape(bsz * seq, d)

    widths = (aw, aw, aw, iw, IDX_DIM, IDX_HEADS, bw, bw, bw, bw, d, d)
    names = ("qa", "ka", "va", "qi", "ki", "wi", "qb", "fb", "ib", "gb", "g_a", "g_b")
    off, start = {}, 0
    for nme, wd in zip(names, widths):
        off[nme] = (start, start + wd)
        start += wd
    order1 = ("g_a", "g_b", "qa", "ka", "va", "qi", "qb", "ib", "gb")
    order2 = ("fb", "ki", "wi")
    pad2 = 128 - IDX_DIM - IDX_HEADS

    for l in range(depth):
        mod = _ada(c, w_ada[l], b_ada[l])
        shift1, scale1, gate1, shift2, scale2, gate2 = [
            m.reshape(bsz, 1, d) for m in jnp.split(mod, 6, axis=-1)]

        wl = w_in[l]
        w1 = jnp.concatenate([wl[:, off[n][0]:off[n][1]] for n in order1], axis=1).astype(BF16)
        w2 = jnp.concatenate([wl[:, off[n][0]:off[n][1]] for n in order2]
                             + [jnp.zeros((d, pad2), wl.dtype)], axis=1).astype(BF16)
        p1 = _inproj(xt, norm_mix[l], scale1, shift1, w1, BF16, seq, tm=1024, tn=1024)
        p2 = _inproj(xt, norm_mix[l], scale1, shift1, w2, F32, seq, tm=1024, tn=w2.shape[1])

        attn = _dsa(p1, p2, bsz, seq, {"qa": 4, "ka": 5, "va": 6, "qi": 7, "kw": 8})
        rec = _hgrn(p1, p2, lower_bounds[l], hgrn_gain[l], bsz, seq,
                    {"qb": 64, "ib": 72, "gb": 80, "fb": 0})

        x1, h2, qp = _mix(attn, rec, p1, xt, gate1,
                          w_up_a[l].astype(BF16), w_up_b[l].astype(BF16), w_out[l].astype(BF16),
                          norm_ffn[l], scale2, shift2, peer_w_q[l].astype(BF16),
                          seq, tm=256, col={"g_a": 0, "g_b": 1})
        keys = peer_keys[l].reshape(2 * PEER_HEADS, PEER_N_KEYS, -1).astype(BF16)
        xt = _peer(qp, keys, h2, peer_u[l].astype(BF16), peer_v[l].T.astype(BF16), x1, gate2,
                   final_norm, seq, tm=512, te=1024)
    return xt.reshape(bsz, seq, d)
```

```python
import functools

import jax
import jax.numpy as jnp
from jax import lax
from jax.experimental import pallas as pl
from jax.experimental.pallas import tpu as pltpu

F32 = jnp.float32
BF16 = jnp.bfloat16

NORM_EPS = 1e-6
CHUNK = 64
TOPK_MAX = 256
A_HEADS = 8
HEAD_DIM = 128
IDX_HEADS = 16
IDX_DIM = 64
B_HEADS = 8
PEER_HEADS = 8
PEER_N_KEYS = 128
PEER_TOPK = 16
SUB = 16

VMEM_LIMIT = 56 * 1024 * 1024
INT_MIN = -2 ** 31

NT_DIMS = (((1,), (1,)), ((), ()))
TN_DIMS = (((0,), (0,)), ((), ()))


def _sigmoid(x):
    return 1.0 / (1.0 + jnp.exp(-x))


def _params(sem):
    return pltpu.CompilerParams(dimension_semantics=sem, vmem_limit_bytes=VMEM_LIMIT)


def _ada_kernel(c_ref, w_ref, b_ref, o_ref):
    c = c_ref[...]
    cond = (c * _sigmoid(c)).astype(BF16)
    o_ref[...] = jnp.dot(cond, w_ref[...].astype(BF16), preferred_element_type=F32) + b_ref[...]


def _ada(c, w, b, tn=1024):
    bsz, d = c.shape
    n = w.shape[1]
    return pl.pallas_call(
        _ada_kernel,
        out_shape=jax.ShapeDtypeStruct((bsz, n), F32),
        grid=(n // tn,),
        in_specs=[pl.BlockSpec((bsz, d), lambda j: (0, 0)),
                  pl.BlockSpec((d, tn), lambda j: (0, j)),
                  pl.BlockSpec((1, tn), lambda j: (0, j))],
        out_specs=pl.BlockSpec((bsz, tn), lambda j: (0, j)),
        compiler_params=_params(("arbitrary",)),
        name="ada",
    )(c, w, b.reshape(1, n))


def _norm_mod(x, gain, scale, shift):
    ms = jnp.mean(x * x, axis=-1, keepdims=True)
    y = x * lax.rsqrt(ms + NORM_EPS) * gain
    return y * (1.0 + scale) + shift


def _inproj_kernel(x_ref, g_ref, sc_ref, sh_ref, w_ref, o_ref, h_ref):
    @pl.when(pl.program_id(1) == 0)
    def _():
        h_ref[...] = _norm_mod(x_ref[...], g_ref[...], sc_ref[0], sh_ref[0]).astype(BF16)

    o_ref[...] = jnp.dot(h_ref[...], w_ref[...], preferred_element_type=F32).astype(o_ref.dtype)


def _inproj(x2d, gain, scale, shift, w, out_dtype, seq, tm, tn):
    t, d = x2d.shape
    n = w.shape[1]
    per_b = seq // tm
    return pl.pallas_call(
        _inproj_kernel,
        out_shape=jax.ShapeDtypeStruct((t, n), out_dtype),
        grid=(t // tm, n // tn),
        in_specs=[pl.BlockSpec((tm, d), lambda i, j: (i, 0)),
                  pl.BlockSpec((1, d), lambda i, j: (0, 0)),
                  pl.BlockSpec((1, 1, d), lambda i, j: (i // per_b, 0, 0)),
                  pl.BlockSpec((1, 1, d), lambda i, j: (i // per_b, 0, 0)),
                  pl.BlockSpec((d, tn), lambda i, j: (0, j))],
        out_specs=pl.BlockSpec((tm, tn), lambda i, j: (i, j)),
        scratch_shapes=[pltpu.VMEM((tm, d), BF16)],
        compiler_params=_params(("arbitrary", "arbitrary")),
        name="inproj",
    )(x2d, gain.reshape(1, d), scale, shift, w)


DSA_QB = 256
DSA_KT = 256
DSA_AT = 128
NEG_BIG = -1e30


def _dsa_kernel(qi_ref, kw_ref, qa_ref, ka_ref, va_ref, o_ref,
                key_ref, vt_ref, m_ref, l_ref, acc_ref):
    j = pl.program_id(1)
    qb, kt_rows = DSA_QB, DSA_KT
    seq = ka_ref.shape[0]
    n_kt = (j * qb + qb + kt_rows - 1) // kt_rows
    r0 = pl.multiple_of(j * qb, qb)

    @pl.when(j == 0)
    def _():
        for h in range(A_HEADS):
            for t in range(seq // kt_rows):
                blk = va_ref[t * kt_rows:(t + 1) * kt_rows, h * HEAD_DIM:(h + 1) * HEAD_DIM]
                vt_ref[h * HEAD_DIM:(h + 1) * HEAD_DIM, t * kt_rows:(t + 1) * kt_rows] = (
                    blk.astype(F32).T.astype(BF16))

    w_t = kw_ref[pl.ds(r0, qb), :].T
    q_chunk = (r0 + lax.broadcasted_iota(jnp.int32, (kt_rows, qb), 1)) // CHUNK
    key_iota = lax.broadcasted_iota(jnp.int32, (kt_rows, qb), 0)

    def score_tile(kt, carry):
        k0 = pl.multiple_of(kt * kt_rows, kt_rows)
        ki = kw_ref[pl.ds(k0, kt_rows), 0:IDX_DIM].astype(BF16)
        sc = jnp.zeros((kt_rows, qb), F32)
        for h in range(IDX_HEADS):
            dots = lax.dot_general(ki, qi_ref[:, h * IDX_DIM:(h + 1) * IDX_DIM], NT_DIMS,
                                   preferred_element_type=F32)
            sc = sc + w_t[IDX_DIM + h:IDX_DIM + h + 1, :] * jnp.maximum(dots, 0.0)
        sc = sc * ((IDX_DIM * IDX_HEADS) ** -0.5)
        bits = pltpu.bitcast(sc, jnp.int32)
        skey = bits ^ ((bits >> 31) & jnp.int32(0x7FFFFFFF))
        adm = (k0 + key_iota) // CHUNK <= q_chunk
        key_ref[pl.ds(k0, kt_rows), :] = jnp.where(adm, skey, jnp.int32(INT_MIN))
        return carry

    lax.fori_loop(0, n_kt, score_tile, 0)

    top_k = float(min(TOPK_MAX, seq // 4))

    def bit_step(it, v):
        cand = v + lax.shift_left(jnp.int32(1), 31 - it)

        def count_tile(kt, part):
            k0 = pl.multiple_of(kt * kt_rows, kt_rows)
            hit = jnp.where(key_ref[pl.ds(k0, kt_rows), :] >= cand, 1.0, 0.0)
            return part + jnp.sum(hit.reshape(kt_rows // 8, 8, qb), axis=0)

        part = lax.fori_loop(0, n_kt, count_tile, jnp.zeros((8, qb), F32))
        cnt = jnp.sum(part, axis=0, keepdims=True)
        return jnp.where(cnt >= top_k, cand, v)

    v = lax.fori_loop(0, 32, bit_step, jnp.full((1, qb), INT_MIN, jnp.int32))
    v = jnp.maximum(v, jnp.int32(INT_MIN + 1))

    att_scale = HEAD_DIM ** -0.5
    m_ref[...] = jnp.full(m_ref.shape, NEG_BIG, F32)
    l_ref[...] = jnp.zeros(l_ref.shape, F32)
    acc_ref[...] = jnp.zeros(acc_ref.shape, F32)

    at_rows = DSA_AT

    def att_tile(kt, carry):
        k0 = pl.multiple_of(kt * at_rows, at_rows)
        bias = jnp.where(key_ref[pl.ds(k0, at_rows), :] >= v, 0.0, NEG_BIG)
        for h in range(A_HEADS):
            cols = slice(h * HEAD_DIM, (h + 1) * HEAD_DIM)
            st = lax.dot_general(ka_ref[pl.ds(k0, at_rows), cols], qa_ref[:, cols], NT_DIMS,
                                 preferred_element_type=F32) * att_scale + bias
            m_old = m_ref[h:h + 1, :]
            m_new = jnp.maximum(m_old, jnp.max(st, axis=0, keepdims=True))
            alpha = jnp.exp(m_old - m_new)
            p = jnp.exp(st - m_new)
            m_ref[h:h + 1, :] = m_new
            l_ref[h:h + 1, :] = l_ref[h:h + 1, :] * alpha + jnp.sum(p, axis=0, keepdims=True)
            acc_ref[cols, :] = acc_ref[cols, :] * alpha + jnp.dot(
                vt_ref[cols, pl.ds(k0, at_rows)], p.astype(BF16), preferred_element_type=F32)
        return carry

    lax.fori_loop(0, n_kt * (kt_rows // at_rows), att_tile, 0)
    for h in range(A_HEADS):
        cols = slice(h * HEAD_DIM, (h + 1) * HEAD_DIM)
        o_ref[:, cols] = (acc_ref[cols, :] / l_ref[h:h + 1, :]).T.astype(o_ref.dtype)


def _dsa(p1, p2, bsz, seq, col):
    n_blk = seq // DSA_QB
    width = A_HEADS * HEAD_DIM
    return pl.pallas_call(
        _dsa_kernel,
        out_shape=jax.ShapeDtypeStruct((bsz * seq, width), BF16),
        grid=(bsz, n_blk),
        in_specs=[pl.BlockSpec((DSA_QB, width), lambda b, k: (b * n_blk + k, col["qi"])),
                  pl.BlockSpec((seq, 128), lambda b, k: (b, col["kw"])),
                  pl.BlockSpec((DSA_QB, width), lambda b, k: (b * n_blk + k, col["qa"])),
                  pl.BlockSpec((seq, width), lambda b, k: (b, col["ka"])),
                  pl.BlockSpec((seq, width), lambda b, k: (b, col["va"]))],
        out_specs=pl.BlockSpec((DSA_QB, width), lambda b, k: (b * n_blk + k, 0)),
        scratch_shapes=[pltpu.VMEM((seq, DSA_QB), jnp.int32), pltpu.VMEM((width, seq), BF16),
                        pltpu.VMEM((A_HEADS, DSA_QB), F32), pltpu.VMEM((A_HEADS, DSA_QB), F32),
                        pltpu.VMEM((width, DSA_QB), F32)],
        compiler_params=_params(("arbitrary", "arbitrary")),
        name="dsa",
    )(p1, p2, p1, p1, p1)


HGRN_GROUP = 4


def _hgrn_kernel(q_ref, f_ref, i_ref, g_ref, lb_ref, gain_ref, o_ref,
                 qe_ref, oi_ref, upd_ref, dec_ref):
    c = CHUNK
    n_c = q_ref.shape[0] // c
    d = q_ref.shape[1]
    lb = lb_ref[0]
    gain = gain_ref[...]
    chunk_row = lax.broadcasted_iota(jnp.int32, (c, d), 0)
    sub_row = lax.broadcasted_iota(jnp.int32, (SUB, d), 0)

    def local_part(ci):
        r0 = pl.multiple_of(ci * c, c)
        f = lb + (1.0 - lb) * _sigmoid(f_ref[pl.ds(r0, c), :])
        logf = jnp.log(f)
        k = 1.0 - f
        qr = q_ref[pl.ds(r0, c), :].astype(F32)
        q = qr * _sigmoid(qr)
        v = i_ref[pl.ds(r0, c), :].astype(F32)
        v16 = v.astype(BF16)
        b = logf
        shift = 1
        while shift < c:
            b = b + jnp.where(chunk_row >= shift, pltpu.roll(b, shift, 0), 0.0)
            shift *= 2
        b_last = b[c - 1:c, :]
        qe_ref[pl.ds(r0, c), :] = (q * jnp.exp(b)).astype(BF16)

        for blk in range(c // SUB):
            lo = blk * SUB
            q_i, k_i, v_i, b_i = q[lo:lo + SUB], k[lo:lo + SUB], v[lo:lo + SUB], b[lo:lo + SUB]
            o_i = jnp.zeros((SUB, d), F32)
            if blk > 0:
                b_ref = b[lo - 1:lo, :]
                qt = (q_i * jnp.exp(b_i - b_ref)).astype(BF16)
                kt = (k[:lo] * jnp.exp(b_ref - b[:lo])).astype(BF16)
                att = lax.dot_general(qt, kt, NT_DIMS, preferred_element_type=F32)
                o_i = jnp.dot(att.astype(BF16), v16[:lo], preferred_element_type=F32)
            for s_ in range(SUB):
                w = jnp.exp(jnp.where(sub_row >= s_, b_i - b_i[s_:s_ + 1, :], -jnp.inf))
                a = jnp.sum(q_i * k_i[s_:s_ + 1, :] * w, axis=1, keepdims=True)
                o_i = o_i + a * v_i[s_:s_ + 1, :]
            oi_ref[pl.ds(r0 + lo, SUB), :] = o_i

        k_end = (k * jnp.exp(b_last - b)).astype(BF16)
        upd_ref[ci] = lax.dot_general(v16, k_end, TN_DIMS, preferred_element_type=F32)
        dec_ref[ci] = jnp.exp(b_last)

    def group(gi, carry):
        for u in range(HGRN_GROUP):
            local_part(gi * HGRN_GROUP + u)
        return carry

    lax.fori_loop(0, n_c // HGRN_GROUP, group, 0)

    st_t = jnp.zeros((d, d), F32)
    for ci in range(n_c):
        rows = slice(ci * c, (ci + 1) * c)
        o = oi_ref[rows, :] + lax.dot_general(qe_ref[rows, :], st_t.astype(BF16), NT_DIMS,
                                              preferred_element_type=F32)
        ms = jnp.mean(o * o, axis=-1, keepdims=True)
        rec = o * lax.rsqrt(ms + NORM_EPS) * gain
        gr = g_ref[rows, :].astype(F32)
        o_ref[rows, :] = (rec * (gr * _sigmoid(gr))).astype(o_ref.dtype)
        st_t = st_t * dec_ref[ci] + upd_ref[ci]


def _hgrn(p1, p2, lb, gain, bsz, seq, col):
    d = HEAD_DIM
    n_c = seq // CHUNK
    return pl.pallas_call(
        _hgrn_kernel,
        out_shape=jax.ShapeDtypeStruct((bsz * seq, B_HEADS * d), BF16),
        grid=(bsz, B_HEADS),
        in_specs=[pl.BlockSpec((seq, d), lambda b, h: (b, col["qb"] + h)),
                  pl.BlockSpec((seq, d), lambda b, h: (b, col["fb"] + h)),
                  pl.BlockSpec((seq, d), lambda b, h: (b, col["ib"] + h)),
                  pl.BlockSpec((seq, d), lambda b, h: (b, col["gb"] + h)),
                  pl.BlockSpec((1, 1, d), lambda b, h: (h, 0, 0)),
                  pl.BlockSpec((1, d), lambda b, h: (0, 0))],
        out_specs=pl.BlockSpec((seq, d), lambda b, h: (b, h)),
        scratch_shapes=[pltpu.VMEM((seq, d), BF16), pltpu.VMEM((seq, d), F32),
                        pltpu.VMEM((n_c, d, d), F32), pltpu.VMEM((n_c, 1, d), F32)],
        compiler_params=_params(("arbitrary", "arbitrary")),
        name="hgrn",
    )(p1, p2, p1, p1, lb.reshape(B_HEADS, 1, d), gain.reshape(1, d))


def _mix_kernel(attn_ref, rec_ref, ga_ref, gb_ref, x_ref, gate_ref, wa_ref, wb_ref, wo_ref,
                g2_ref, sc_ref, sh_ref, wq_ref, x1_ref, h2_ref, qp_ref):
    ya = jnp.dot(attn_ref[...], wa_ref[...], preferred_element_type=F32)
    yb = jnp.dot(rec_ref[...], wb_ref[...], preferred_element_type=F32)
    mixed = _sigmoid(ga_ref[...].astype(F32)) * ya + _sigmoid(gb_ref[...].astype(F32)) * yb
    x1 = x_ref[...] + gate_ref[0] * jnp.dot(mixed.astype(BF16), wo_ref[...],
                                            preferred_element_type=F32)
    x1_ref[...] = x1
    h2 = _norm_mod(x1, g2_ref[...], sc_ref[0], sh_ref[0]).astype(BF16)
    h2_ref[...] = h2
    qp_ref[...] = jnp.dot(h2, wq_ref[...], preferred_element_type=F32).astype(qp_ref.dtype)


def _mix(attn, rec, p1, x2d, gate1, wa, wb, wo, g2, scale2, shift2, wq, seq, tm, col):
    t, d = x2d.shape
    wa_w = attn.shape[1]
    per_b = seq // tm
    full = lambda shape: pl.BlockSpec(shape, lambda i: (0,) * len(shape),
                                      pipeline_mode=pl.Buffered(1))
    mod = pl.BlockSpec((1, 1, d), lambda i: (i // per_b, 0, 0))
    return pl.pallas_call(
        _mix_kernel,
        out_shape=(jax.ShapeDtypeStruct((t, d), F32), jax.ShapeDtypeStruct((t, d), BF16),
                   jax.ShapeDtypeStruct((t, wq.shape[1]), BF16)),
        grid=(t // tm,),
        in_specs=[pl.BlockSpec((tm, wa_w), lambda i: (i, 0)),
                  pl.BlockSpec((tm, wa_w), lambda i: (i, 0)),
                  pl.BlockSpec((tm, d), lambda i: (i, col["g_a"])),
                  pl.BlockSpec((tm, d), lambda i: (i, col["g_b"])),
                  pl.BlockSpec((tm, d), lambda i: (i, 0)),
                  mod, full(wa.shape), full(wb.shape), full(wo.shape),
                  full((1, d)), mod, mod, full(wq.shape)],
        out_specs=(pl.BlockSpec((tm, d), lambda i: (i, 0)),
                   pl.BlockSpec((tm, d), lambda i: (i, 0)),
                   pl.BlockSpec((tm, wq.shape[1]), lambda i: (i, 0))),
        compiler_params=_params(("arbitrary",)),
        name="mix",
    )(attn, rec, p1, p1, x2d, gate1, wa, wb, wo, g2.reshape(1, d), scale2, shift2, wq)


def _top_values(work, n, dst_ref):
    rank = jnp.full(work.shape, float(n), F32)
    for r in range(n):
        mx = jnp.max(work, axis=0, keepdims=True)
        dst_ref[r:r + 1, :] = mx
        hit = work == mx
        rank = jnp.where(hit, float(r), rank)
        work = jnp.where(hit, -jnp.inf, work)
    return rank


def _gelu(x):
    return 0.5 * x * (1.0 + lax.erf(x * (2.0 ** -0.5)))


PEER_LANES = 128


def _peer_stats(qp_ref, keys_ref, m_ref, rank_ref, c0_ref, e1_ref, top_ref, cand_ref, best_ref):
    nk, kk, tl = PEER_N_KEYS, PEER_TOPK, PEER_LANES
    row8 = lax.broadcasted_iota(jnp.int32, (8, tl), 0)

    def chunk(c, carry):
        t0 = pl.multiple_of(c * tl, tl)
        for p in range(PEER_HEADS):
            scores = []
            for half in range(2):
                j = 2 * p + half
                qs = qp_ref[pl.ds(t0, tl), j * nk:(j + 1) * nk]
                scores.append(lax.dot_general(keys_ref[j], qs, NT_DIMS,
                                              preferred_element_type=F32))
            s0, s1 = scores
            _top_values(s0, kk, top_ref.at[0])
            rank_ref[c, p] = _top_values(s1, kk, top_ref.at[1]).astype(BF16)
            cand_ref[0:kk, :] = top_ref[0, 0:1, :] + top_ref[1]
            cand_ref[kk:kk + 8, :] = top_ref[0, 1:2, :] + top_ref[1, 0:8, :]
            for a in range(2, 8):
                grp = top_ref[0, a:a + 1, :] + top_ref[1, 0:8, :]
                cand_ref[8 + 8 * a:16 + 8 * a, :] = jnp.where(row8 < kk // (a + 1), grp, -jnp.inf)
            cand_ref[72:80, :] = top_ref[0, 8:16, :] + top_ref[1, 0:1, :]
            _top_values(cand_ref[...], kk, best_ref)
            best = best_ref[...]
            z = jnp.sum(jnp.exp(best - best[0:1, :]), axis=0, keepdims=True)
            thr = best[kk - 1:kk, :]
            m = jnp.zeros(s0.shape, F32)
            for b in range(kk):
                m = m + jnp.where(s0 + top_ref[1, b:b + 1, :] >= thr, 1.0, 0.0)
            m_ref[c, p] = m
            c0_ref[c, p] = jnp.exp(s0 - top_ref[0, 0:1, :]) / z
            e1_ref[c, p] = jnp.exp(s1 - top_ref[1, 0:1, :]).astype(BF16)
        return carry

    lax.fori_loop(0, m_ref.shape[0], chunk, 0)


def _peer_kernel(qp_ref, keys_ref, h2_ref, u_ref, vt_ref, x1_ref, gate_ref, fin_ref, o_ref,
                 m_ref, rank_ref, c0_ref, e1_ref, acc_ref, top_ref, cand_ref, best_ref, act_ref):
    eb = pl.program_id(1)
    n_eb = pl.num_programs(1)
    nk, tl = PEER_N_KEYS, PEER_LANES
    rows = u_ref.shape[0] // nk

    @pl.when(eb == 0)
    def _():
        acc_ref[...] = jnp.zeros_like(acc_ref)
        _peer_stats(qp_ref, keys_ref, m_ref, rank_ref, c0_ref, e1_ref, top_ref, cand_ref, best_ref)

    a_t = lax.dot_general(u_ref[...], h2_ref[...], NT_DIMS, preferred_element_type=F32)
    for r in range(rows):
        i = eb * rows + r
        sl = slice(r * nk, (r + 1) * nk)
        for c in range(m_ref.shape[0]):
            lanes = slice(c * tl, (c + 1) * tl)
            gmat = jnp.zeros((nk, tl), BF16)
            for p in range(PEER_HEADS):
                c0row = jnp.broadcast_to(c0_ref[c, p, pl.ds(i, 1), :].astype(BF16), (nk, tl))
                mrow = jnp.broadcast_to(m_ref[c, p, pl.ds(i, 1), :].astype(BF16), (nk, tl))
                val = c0row * e1_ref[c, p]
                gmat = gmat + jnp.where(rank_ref[c, p] < mrow, val, jnp.zeros_like(val))
            act_ref[sl, lanes] = _gelu(a_t[sl, lanes]).astype(BF16) * gmat
    acc_ref[...] += jnp.dot(vt_ref[0], act_ref[...], preferred_element_type=F32)

    @pl.when(eb == n_eb - 1)
    def _():
        x2 = x1_ref[...] + gate_ref[0] * acc_ref[...].T
        ms = jnp.mean(x2 * x2, axis=-1, keepdims=True)
        o_ref[...] = x2 * lax.rsqrt(ms + NORM_EPS) * fin_ref[...]


def _peer(qp, keys, h2, u, vt, x1, gate2, fin, seq, tm):
    t, d = x1.shape
    te = vt.shape[2]
    per_b = seq // tm
    nk, tl = PEER_N_KEYS, PEER_LANES
    stat = pltpu.VMEM((tm // tl, PEER_HEADS, nk, tl), F32)
    stat16 = pltpu.VMEM((tm // tl, PEER_HEADS, nk, tl), BF16)
    once = pl.Buffered(1)
    return pl.pallas_call(
        _peer_kernel,
        out_shape=jax.ShapeDtypeStruct((t, d), F32),
        grid=(t // tm, u.shape[0] // te),
        in_specs=[pl.BlockSpec((tm, qp.shape[1]), lambda i, e: (i, 0), pipeline_mode=once),
                  pl.BlockSpec(keys.shape, lambda i, e: (0, 0, 0), pipeline_mode=once),
                  pl.BlockSpec((tm, d), lambda i, e: (i, 0)),
                  pl.BlockSpec((te, d), lambda i, e: (e, 0)),
                  pl.BlockSpec((1, d, te), lambda i, e: (e, 0, 0)),
                  pl.BlockSpec((tm, d), lambda i, e: (i, 0), pipeline_mode=once),
                  pl.BlockSpec((1, 1, d), lambda i, e: (i // per_b, 0, 0)),
                  pl.BlockSpec((1, d), lambda i, e: (0, 0))],
        out_specs=pl.BlockSpec((tm, d), lambda i, e: (i, 0)),
        scratch_shapes=[stat, stat16, stat, stat16,
                        pltpu.VMEM((d, tm), F32),
                        pltpu.VMEM((2, PEER_TOPK, tl), F32),
                        pltpu.VMEM((80, tl), F32),
                        pltpu.VMEM((PEER_TOPK, tl), F32),
                        pltpu.VMEM((te, tm), BF16)],
        compiler_params=_params(("arbitrary", "arbitrary")),
        name="peer",
    )(qp, keys, h2, u, vt, x1, gate2, fin.reshape(1, d))


def kernel(x, c, w_ada, b_ada, norm_mix, norm_ffn, w_in, lb_logits, hgrn_gain, w_up_a, w_up_b,
           w_out, peer_w_q, peer_keys, peer_u, peer_v, final_norm):
    bsz, seq, d = x.shape
    depth = w_ada.shape[0]
    assert depth == 1, "the PEER kernel applies the final norm, so it must be the last layer"
    aw = A_HEADS * HEAD_DIM
    bw = B_HEADS * HEAD_DIM
    iw = IDX_HEADS * IDX_DIM
    lower_bounds = jnp.cumsum(jax.nn.softmax(lb_logits.astype(F32), axis=0), axis=0)
    xt = x.reshape(bsz * seq, d)

    widths = (aw, aw, aw, iw, IDX_DIM, IDX_HEADS, bw, bw, bw, bw, d, d)
    names = ("qa", "ka", "va", "qi", "ki", "wi", "qb", "fb", "ib", "gb", "g_a", "g_b")
    off, start = {}, 0
    for nme, wd in zip(names, widths):
        off[nme] = (start, start + wd)
        start += wd
    order1 = ("g_a", "g_b", "qa", "ka", "va", "qi", "qb", "ib", "gb")
    order2 = ("fb", "ki", "wi")
    pad2 = 128 - IDX_DIM - IDX_HEADS

    for l in range(depth):
        mod = _ada(c, w_ada[l], b_ada[l])
        shift1, scale1, gate1, shift2, scale2, gate2 = [
            m.reshape(bsz, 1, d) for m in jnp.split(mod, 6, axis=-1)]

        wl = w_in[l]
        w1 = jnp.concatenate([wl[:, off[n][0]:off[n][1]] for n in order1], axis=1).astype(BF16)
        w2 = jnp.concatenate([wl[:, off[n][0]:off[n][1]] for n in order2]
                             + [jnp.zeros((d, pad2), wl.dtype)], axis=1).astype(BF16)
        p1 = _inproj(xt, norm_mix[l], scale1, shift1, w1, BF16, seq, tm=1024, tn=1024)
        p2 = _inproj(xt, norm_mix[l], scale1, shift1, w2, F32, seq, tm=1024, tn=w2.shape[1])

        attn = _dsa(p1, p2, bsz, seq, {"qa": 4, "ka": 5, "va": 6, "qi": 7, "kw": 8})
        rec = _hgrn(p1, p2, lower_bounds[l], hgrn_gain[l], bsz, seq,
                    {"qb": 64, "ib": 72, "gb": 80, "fb": 0})

        x1, h2, qp = _mix(attn, rec, p1, xt, gate1,
                          w_up_a[l].astype(BF16), w_up_b[l].astype(BF16), w_out[l].astype(BF16),
                          norm_ffn[l], scale2, shift2, peer_w_q[l].astype(BF16),
                          seq, tm=256, col={"g_a": 0, "g_b": 1})
        keys = peer_keys[l].reshape(2 * PEER_HEADS, PEER_N_KEYS, -1).astype(BF16)
        te = 1024
        vt = peer_v[l].astype(BF16).reshape(-1, te, d).transpose(0, 2, 1)
        xt = _peer(qp, keys, h2, peer_u[l].astype(BF16), vt, x1, gate2, final_norm, seq, tm=512)
    return xt.reshape(bsz, seq, d)
```

```python
import functools

import jax
import jax.numpy as jnp
from jax import lax
from jax.experimental import pallas as pl
from jax.experimental.pallas import tpu as pltpu

F32 = jnp.float32
BF16 = jnp.bfloat16

NORM_EPS = 1e-6
CHUNK = 64
TOPK_MAX = 256
A_HEADS = 8
HEAD_DIM = 128
IDX_HEADS = 16
IDX_DIM = 64
B_HEADS = 8
PEER_HEADS = 8
PEER_N_KEYS = 128
PEER_TOPK = 16
SUB = 16

VMEM_LIMIT = 56 * 1024 * 1024
INT_MIN = -2 ** 31

NT_DIMS = (((1,), (1,)), ((), ()))
TN_DIMS = (((0,), (0,)), ((), ()))


def _sigmoid(x):
    return 1.0 / (1.0 + jnp.exp(-x))


def _params(sem):
    return pltpu.CompilerParams(dimension_semantics=sem, vmem_limit_bytes=VMEM_LIMIT)


def _ada_kernel(c_ref, w_ref, b_ref, o_ref):
    c = c_ref[...]
    cond = (c * _sigmoid(c)).astype(BF16)
    o_ref[...] = jnp.dot(cond, w_ref[...].astype(BF16), preferred_element_type=F32) + b_ref[...]


def _ada(c, w, b, tn=1024):
    bsz, d = c.shape
    n = w.shape[1]
    return pl.pallas_call(
        _ada_kernel,
        out_shape=jax.ShapeDtypeStruct((bsz, n), F32),
        grid=(n // tn,),
        in_specs=[pl.BlockSpec((bsz, d), lambda j: (0, 0)),
                  pl.BlockSpec((d, tn), lambda j: (0, j)),
                  pl.BlockSpec((1, tn), lambda j: (0, j))],
        out_specs=pl.BlockSpec((bsz, tn), lambda j: (0, j)),
        compiler_params=_params(("arbitrary",)),
        name="ada",
    )(c, w, b.reshape(1, n))


def _norm_mod(x, gain, scale, shift):
    ms = jnp.mean(x * x, axis=-1, keepdims=True)
    y = x * lax.rsqrt(ms + NORM_EPS) * gain
    return y * (1.0 + scale) + shift


def _inproj_kernel(x_ref, g_ref, sc_ref, sh_ref, w1_ref, w2_ref, o1_ref, o2_ref, h_ref):
    j = pl.program_id(1)
    n1 = pl.num_programs(1) - 1

    @pl.when(j == 0)
    def _():
        h_ref[...] = _norm_mod(x_ref[...], g_ref[...], sc_ref[0], sh_ref[0]).astype(BF16)

    @pl.when(j < n1)
    def _():
        o1_ref[...] = jnp.dot(h_ref[...], w1_ref[...],
                              preferred_element_type=F32).astype(o1_ref.dtype)

    @pl.when(j == n1)
    def _():
        o2_ref[...] = jnp.dot(h_ref[...], w2_ref[...], preferred_element_type=F32)


def _inproj(x2d, gain, scale, shift, w1, w2, seq, tm, tn):
    t, d = x2d.shape
    n1 = w1.shape[1] // tn
    n2 = w2.shape[1]
    per_b = seq // tm
    return pl.pallas_call(
        _inproj_kernel,
        out_shape=(jax.ShapeDtypeStruct((t, w1.shape[1]), BF16),
                   jax.ShapeDtypeStruct((t, n2), F32)),
        grid=(t // tm, n1 + 1),
        in_specs=[pl.BlockSpec((tm, d), lambda i, j: (i, 0)),
                  pl.BlockSpec((1, d), lambda i, j: (0, 0)),
                  pl.BlockSpec((1, 1, d), lambda i, j: (i // per_b, 0, 0)),
                  pl.BlockSpec((1, 1, d), lambda i, j: (i // per_b, 0, 0)),
                  pl.BlockSpec((d, tn), lambda i, j: (0, jnp.minimum(j, n1 - 1))),
                  pl.BlockSpec((d, n2), lambda i, j: (0, 0))],
        out_specs=(pl.BlockSpec((tm, tn), lambda i, j: (i, jnp.minimum(j, n1 - 1))),
                   pl.BlockSpec((tm, n2), lambda i, j: (i, 0))),
        scratch_shapes=[pltpu.VMEM((tm, d), BF16)],
        compiler_params=_params(("arbitrary", "arbitrary")),
        name="inproj",
    )(x2d, gain.reshape(1, d), scale, shift, w1, w2)


DSA_QB = 256
DSA_KT = 256
DSA_AT = 128
NEG_BIG = -1e30


def _dsa_kernel(qi_ref, kw_ref, qa_ref, ka_ref, va_ref, o_ref,
                key_ref, vt_ref, m_ref, l_ref, acc_ref):
    j = pl.program_id(1)
    qb, kt_rows = DSA_QB, DSA_KT
    seq = ka_ref.shape[0]
    n_kt = (j * qb + qb + kt_rows - 1) // kt_rows
    r0 = pl.multiple_of(j * qb, qb)

    @pl.when(j == 0)
    def _():
        for h in range(A_HEADS):
            for t in range(seq // kt_rows):
                blk = va_ref[t * kt_rows:(t + 1) * kt_rows, h * HEAD_DIM:(h + 1) * HEAD_DIM]
                vt_ref[h * HEAD_DIM:(h + 1) * HEAD_DIM, t * kt_rows:(t + 1) * kt_rows] = (
                    blk.astype(F32).T.astype(BF16))

    w_t = kw_ref[pl.ds(r0, qb), :].T
    q_chunk = (r0 + lax.broadcasted_iota(jnp.int32, (kt_rows, qb), 1)) // CHUNK
    key_iota = lax.broadcasted_iota(jnp.int32, (kt_rows, qb), 0)

    def score_tile(kt, carry):
        k0 = pl.multiple_of(kt * kt_rows, kt_rows)
        ki = kw_ref[pl.ds(k0, kt_rows), 0:IDX_DIM].astype(BF16)
        sc = jnp.zeros((kt_rows, qb), F32)
        for h in range(IDX_HEADS):
            dots = lax.dot_general(ki, qi_ref[:, h * IDX_DIM:(h + 1) * IDX_DIM], NT_DIMS,
                                   preferred_element_type=F32)
            sc = sc + w_t[IDX_DIM + h:IDX_DIM + h + 1, :] * jnp.maximum(dots, 0.0)
        sc = sc * ((IDX_DIM * IDX_HEADS) ** -0.5)
        bits = pltpu.bitcast(sc, jnp.int32)
        skey = bits ^ ((bits >> 31) & jnp.int32(0x7FFFFFFF))
        adm = (k0 + key_iota) // CHUNK <= q_chunk
        key_ref[pl.ds(k0, kt_rows), :] = jnp.where(adm, skey, jnp.int32(INT_MIN))
        return carry

    lax.fori_loop(0, n_kt, score_tile, 0)

    top_k = float(min(TOPK_MAX, seq // 4))

    def bit_step(it, v):
        cand = v + lax.shift_left(jnp.int32(1), 31 - it)

        def count_tile(kt, part):
            k0 = pl.multiple_of(kt * kt_rows, kt_rows)
            hit = jnp.where(key_ref[pl.ds(k0, kt_rows), :] >= cand, 1.0, 0.0)
            return part + jnp.sum(hit.reshape(kt_rows // 8, 8, qb), axis=0)

        part = lax.fori_loop(0, n_kt, count_tile, jnp.zeros((8, qb), F32))
        cnt = jnp.sum(part, axis=0, keepdims=True)
        return jnp.where(cnt >= top_k, cand, v)

    v = lax.fori_loop(0, 32, bit_step, jnp.full((1, qb), INT_MIN, jnp.int32))
    v = jnp.maximum(v, jnp.int32(INT_MIN + 1))

    att_scale = HEAD_DIM ** -0.5
    m_ref[...] = jnp.full(m_ref.shape, NEG_BIG, F32)
    l_ref[...] = jnp.zeros(l_ref.shape, F32)
    acc_ref[...] = jnp.zeros(acc_ref.shape, F32)

    at_rows = DSA_AT

    def att_tile(kt, carry):
        k0 = pl.multiple_of(kt * at_rows, at_rows)
        bias = jnp.where(key_ref[pl.ds(k0, at_rows), :] >= v, 0.0, NEG_BIG)
        for h in range(A_HEADS):
            cols = slice(h * HEAD_DIM, (h + 1) * HEAD_DIM)
            st = lax.dot_general(ka_ref[pl.ds(k0, at_rows), cols], qa_ref[:, cols], NT_DIMS,
                                 preferred_element_type=F32) * att_scale + bias
            m_old = m_ref[h:h + 1, :]
            m_new = jnp.maximum(m_old, jnp.max(st, axis=0, keepdims=True))
            alpha = jnp.exp(m_old - m_new)
            p = jnp.exp(st - m_new)
            m_ref[h:h + 1, :] = m_new
            l_ref[h:h + 1, :] = l_ref[h:h + 1, :] * alpha + jnp.sum(p, axis=0, keepdims=True)
            acc_ref[cols, :] = acc_ref[cols, :] * alpha + jnp.dot(
                vt_ref[cols, pl.ds(k0, at_rows)], p.astype(BF16), preferred_element_type=F32)
        return carry

    lax.fori_loop(0, n_kt * (kt_rows // at_rows), att_tile, 0)
    for h in range(A_HEADS):
        cols = slice(h * HEAD_DIM, (h + 1) * HEAD_DIM)
        o_ref[:, cols] = (acc_ref[cols, :] / l_ref[h:h + 1, :]).T.astype(o_ref.dtype)


def _dsa(p1, p2, bsz, seq, col):
    n_blk = seq // DSA_QB
    width = A_HEADS * HEAD_DIM
    return pl.pallas_call(
        _dsa_kernel,
        out_shape=jax.ShapeDtypeStruct((bsz * seq, width), BF16),
        grid=(bsz, n_blk),
        in_specs=[pl.BlockSpec((DSA_QB, width), lambda b, k: (b * n_blk + k, col["qi"])),
                  pl.BlockSpec((seq, 128), lambda b, k: (b, col["kw"])),
                  pl.BlockSpec((DSA_QB, width), lambda b, k: (b * n_blk + k, col["qa"])),
                  pl.BlockSpec((seq, width), lambda b, k: (b, col["ka"])),
                  pl.BlockSpec((seq, width), lambda b, k: (b, col["va"]))],
        out_specs=pl.BlockSpec((DSA_QB, width), lambda b, k: (b * n_blk + k, 0)),
        scratch_shapes=[pltpu.VMEM((seq, DSA_QB), jnp.int32), pltpu.VMEM((width, seq), BF16),
                        pltpu.VMEM((A_HEADS, DSA_QB), F32), pltpu.VMEM((A_HEADS, DSA_QB), F32),
                        pltpu.VMEM((width, DSA_QB), F32)],
        compiler_params=_params(("arbitrary", "arbitrary")),
        name="dsa",
    )(p1, p2, p1, p1, p1)


HGRN_GROUP = 4


def _hgrn_kernel(q_ref, f_ref, i_ref, g_ref, lb_ref, gain_ref, o_ref,
                 qe_ref, oi_ref, upd_ref, dec_ref):
    c = CHUNK
    n_c = q_ref.shape[0] // c
    d = q_ref.shape[1]
    lb = lb_ref[0]
    gain = gain_ref[...]
    chunk_row = lax.broadcasted_iota(jnp.int32, (c, d), 0)
    sub_row = lax.broadcasted_iota(jnp.int32, (SUB, d), 0)

    def local_part(ci):
        r0 = pl.multiple_of(ci * c, c)
        f = lb + (1.0 - lb) * _sigmoid(f_ref[pl.ds(r0, c), :])
        logf = jnp.log(f)
        k = 1.0 - f
        qr = q_ref[pl.ds(r0, c), :].astype(F32)
        q = qr * _sigmoid(qr)
        v = i_ref[pl.ds(r0, c), :].astype(F32)
        v16 = v.astype(BF16)
        b = logf
        shift = 1
        while shift < c:
            b = b + jnp.where(chunk_row >= shift, pltpu.roll(b, shift, 0), 0.0)
            shift *= 2
        b_last = b[c - 1:c, :]
        qe_ref[pl.ds(r0, c), :] = (q * jnp.exp(b)).astype(BF16)

        for blk in range(c // SUB):
            lo = blk * SUB
            q_i, k_i, v_i, b_i = q[lo:lo + SUB], k[lo:lo + SUB], v[lo:lo + SUB], b[lo:lo + SUB]
            o_i = jnp.zeros((SUB, d), F32)
            if blk > 0:
                b_ref = b[lo - 1:lo, :]
                qt = (q_i * jnp.exp(b_i - b_ref)).astype(BF16)
                kt = (k[:lo] * jnp.exp(b_ref - b[:lo])).astype(BF16)
                att = lax.dot_general(qt, kt, NT_DIMS, preferred_element_type=F32)
                o_i = jnp.dot(att.astype(BF16), v16[:lo], preferred_element_type=F32)
            for s_ in range(SUB):
                w = jnp.exp(jnp.where(sub_row >= s_, b_i - b_i[s_:s_ + 1, :], -jnp.inf))
                a = jnp.sum(q_i * k_i[s_:s_ + 1, :] * w, axis=1, keepdims=True)
                o_i = o_i + a * v_i[s_:s_ + 1, :]
            oi_ref[pl.ds(r0 + lo, SUB), :] = o_i

        k_end = (k * jnp.exp(b_last - b)).astype(BF16)
        upd_ref[ci] = lax.dot_general(v16, k_end, TN_DIMS, preferred_element_type=F32)
        dec_ref[ci] = jnp.exp(b_last)

    def group(gi, carry):
        for u in range(HGRN_GROUP):
            local_part(gi * HGRN_GROUP + u)
        return carry

    lax.fori_loop(0, n_c // HGRN_GROUP, group, 0)

    st_t = jnp.zeros((d, d), F32)
    for ci in range(n_c):
        rows = slice(ci * c, (ci + 1) * c)
        o = oi_ref[rows, :] + lax.dot_general(qe_ref[rows, :], st_t.astype(BF16), NT_DIMS,
                                              preferred_element_type=F32)
        ms = jnp.mean(o * o, axis=-1, keepdims=True)
        rec = o * lax.rsqrt(ms + NORM_EPS) * gain
        gr = g_ref[rows, :].astype(F32)
        o_ref[rows, :] = (rec * (gr * _sigmoid(gr))).astype(o_ref.dtype)
        st_t = st_t * dec_ref[ci] + upd_ref[ci]


def _hgrn(p1, p2, lb, gain, bsz, seq, col):
    d = HEAD_DIM
    n_c = seq // CHUNK
    return pl.pallas_call(
        _hgrn_kernel,
        out_shape=jax.ShapeDtypeStruct((bsz * seq, B_HEADS * d), BF16),
        grid=(bsz, B_HEADS),
        in_specs=[pl.BlockSpec((seq, d), lambda b, h: (b, col["qb"] + h)),
                  pl.BlockSpec((seq, d), lambda b, h: (b, col["fb"] + h)),
                  pl.BlockSpec((seq, d), lambda b, h: (b, col["ib"] + h)),
                  pl.BlockSpec((seq, d), lambda b, h: (b, col["gb"] + h)),
                  pl.BlockSpec((1, 1, d), lambda b, h: (h, 0, 0)),
                  pl.BlockSpec((1, d), lambda b, h: (0, 0))],
        out_specs=pl.BlockSpec((seq, d), lambda b, h: (b, h)),
        scratch_shapes=[pltpu.VMEM((seq, d), BF16), pltpu.VMEM((seq, d), F32),
                        pltpu.VMEM((n_c, d, d), F32), pltpu.VMEM((n_c, 1, d), F32)],
        compiler_params=_params(("arbitrary", "arbitrary")),
        name="hgrn",
    )(p1, p2, p1, p1, lb.reshape(B_HEADS, 1, d), gain.reshape(1, d))


def _mix_kernel(attn_ref, rec_ref, ga_ref, gb_ref, x_ref, gate_ref, wa_ref, wb_ref, wo_ref,
                g2_ref, sc_ref, sh_ref, wq_ref, x1_ref, h2_ref, qp_ref):
    ya = jnp.dot(attn_ref[...], wa_ref[...], preferred_element_type=F32)
    yb = jnp.dot(rec_ref[...], wb_ref[...], preferred_element_type=F32)
    mixed = _sigmoid(ga_ref[...].astype(F32)) * ya + _sigmoid(gb_ref[...].astype(F32)) * yb
    x1 = x_ref[...] + gate_ref[0] * jnp.dot(mixed.astype(BF16), wo_ref[...],
                                            preferred_element_type=F32)
    x1_ref[...] = x1
    h2 = _norm_mod(x1, g2_ref[...], sc_ref[0], sh_ref[0]).astype(BF16)
    h2_ref[...] = h2
    qp_ref[...] = jnp.dot(h2, wq_ref[...], preferred_element_type=F32).astype(qp_ref.dtype)


def _mix(attn, rec, p1, x2d, gate1, wa, wb, wo, g2, scale2, shift2, wq, seq, tm, col):
    t, d = x2d.shape
    wa_w = attn.shape[1]
    per_b = seq // tm
    full = lambda shape: pl.BlockSpec(shape, lambda i: (0,) * len(shape),
                                      pipeline_mode=pl.Buffered(1))
    mod = pl.BlockSpec((1, 1, d), lambda i: (i // per_b, 0, 0))
    return pl.pallas_call(
        _mix_kernel,
        out_shape=(jax.ShapeDtypeStruct((t, d), F32), jax.ShapeDtypeStruct((t, d), BF16),
                   jax.ShapeDtypeStruct((t, wq.shape[1]), BF16)),
        grid=(t // tm,),
        in_specs=[pl.BlockSpec((tm, wa_w), lambda i: (i, 0)),
                  pl.BlockSpec((tm, wa_w), lambda i: (i, 0)),
                  pl.BlockSpec((tm, d), lambda i: (i, col["g_a"])),
                  pl.BlockSpec((tm, d), lambda i: (i, col["g_b"])),
                  pl.BlockSpec((tm, d), lambda i: (i, 0)),
                  mod, full(wa.shape), full(wb.shape), full(wo.shape),
                  full((1, d)), mod, mod, full(wq.shape)],
        out_specs=(pl.BlockSpec((tm, d), lambda i: (i, 0)),
                   pl.BlockSpec((tm, d), lambda i: (i, 0)),
                   pl.BlockSpec((tm, wq.shape[1]), lambda i: (i, 0))),
        compiler_params=_params(("arbitrary",)),
        name="mix",
    )(attn, rec, p1, p1, x2d, gate1, wa, wb, wo, g2.reshape(1, d), scale2, shift2, wq)


def _top_values(work, n, dst_ref):
    rank = jnp.full(work.shape, float(n), F32)
    for r in range(n):
        mx = jnp.max(work, axis=0, keepdims=True)
        dst_ref[r:r + 1, :] = mx
        hit = work == mx
        rank = jnp.where(hit, float(r), rank)
        work = jnp.where(hit, -jnp.inf, work)
    return rank


def _gelu(x):
    return 0.5 * x * (1.0 + lax.erf(x * (2.0 ** -0.5)))


PEER_LANES = 128


def _peer_stats(qp_ref, keys_ref, m_ref, rank_ref, c0_ref, e1_ref, top_ref, cand_ref, best_ref):
    nk, kk, tl = PEER_N_KEYS, PEER_TOPK, PEER_LANES
    row8 = lax.broadcasted_iota(jnp.int32, (8, tl), 0)

    def chunk(c, carry):
        t0 = pl.multiple_of(c * tl, tl)
        for p in range(PEER_HEADS):
            scores = []
            for half in range(2):
                j = 2 * p + half
                qs = qp_ref[pl.ds(t0, tl), j * nk:(j + 1) * nk]
                scores.append(lax.dot_general(keys_ref[j], qs, NT_DIMS,
                                              preferred_element_type=F32))
            s0, s1 = scores
            _top_values(s0, kk, top_ref.at[0])
            rank_ref[c, p] = _top_values(s1, kk, top_ref.at[1]).astype(BF16)
            cand_ref[0:kk, :] = top_ref[0, 0:1, :] + top_ref[1]
            cand_ref[kk:kk + 8, :] = top_ref[0, 1:2, :] + top_ref[1, 0:8, :]
            for a in range(2, 8):
                grp = top_ref[0, a:a + 1, :] + top_ref[1, 0:8, :]
                cand_ref[8 + 8 * a:16 + 8 * a, :] = jnp.where(row8 < kk // (a + 1), grp, -jnp.inf)
            cand_ref[72:80, :] = top_ref[0, 8:16, :] + top_ref[1, 0:1, :]
            _top_values(cand_ref[...], kk, best_ref)
            best = best_ref[...]
            z = jnp.sum(jnp.exp(best - best[0:1, :]), axis=0, keepdims=True)
            thr = best[kk - 1:kk, :]
            m = jnp.zeros(s0.shape, F32)
            for b in range(4):
                m = m + jnp.where(s0 + top_ref[1, b:b + 1, :] >= thr, 1.0, 0.0)
            for a in range(3):
                tail = top_ref[0, a:a + 1, :] + top_ref[1, 4:kk // (a + 1), :]
                extra = jnp.sum(jnp.where(tail >= thr, 1.0, 0.0), axis=0, keepdims=True)
                m = m + jnp.where(s0 == top_ref[0, a:a + 1, :], extra, 0.0)
            m_ref[c, p] = m
            c0_ref[c, p] = jnp.exp(s0 - top_ref[0, 0:1, :]) / z
            e1_ref[c, p] = jnp.exp(s1 - top_ref[1, 0:1, :]).astype(BF16)
        return carry

    lax.fori_loop(0, m_ref.shape[0], chunk, 0)


def _peer_kernel(qp_ref, keys_ref, h2_ref, u_ref, vt_ref, x1_ref, gate_ref, fin_ref, o_ref,
                 m_ref, rank_ref, c0_ref, e1_ref, acc_ref, top_ref, cand_ref, best_ref, act_ref):
    eb = pl.program_id(1)
    n_eb = pl.num_programs(1)
    nk, tl = PEER_N_KEYS, PEER_LANES
    rows = u_ref.shape[0] // nk

    @pl.when(eb == 0)
    def _():
        acc_ref[...] = jnp.zeros_like(acc_ref)
        _peer_stats(qp_ref, keys_ref, m_ref, rank_ref, c0_ref, e1_ref, top_ref, cand_ref, best_ref)

    a_t = lax.dot_general(u_ref[...], h2_ref[...], NT_DIMS, preferred_element_type=F32)
    for r in range(rows):
        i = eb * rows + r
        sl = slice(r * nk, (r + 1) * nk)
        for c in range(m_ref.shape[0]):
            lanes = slice(c * tl, (c + 1) * tl)
            gmat = jnp.zeros((nk, tl), BF16)
            for p in range(PEER_HEADS):
                c0row = jnp.broadcast_to(c0_ref[c, p, pl.ds(i, 1), :].astype(BF16), (nk, tl))
                mrow = jnp.broadcast_to(m_ref[c, p, pl.ds(i, 1), :].astype(BF16), (nk, tl))
                val = c0row * e1_ref[c, p]
                gmat = gmat + jnp.where(rank_ref[c, p] < mrow, val, jnp.zeros_like(val))
            act_ref[sl, lanes] = _gelu(a_t[sl, lanes]).astype(BF16) * gmat
    acc_ref[...] += jnp.dot(vt_ref[0], act_ref[...], preferred_element_type=F32)

    @pl.when(eb == n_eb - 1)
    def _():
        x2 = x1_ref[...] + gate_ref[0] * acc_ref[...].T
        ms = jnp.mean(x2 * x2, axis=-1, keepdims=True)
        o_ref[...] = x2 * lax.rsqrt(ms + NORM_EPS) * fin_ref[...]


def _peer(qp, keys, h2, u, vt, x1, gate2, fin, seq, tm):
    t, d = x1.shape
    te = vt.shape[2]
    per_b = seq // tm
    nk, tl = PEER_N_KEYS, PEER_LANES
    stat = pltpu.VMEM((tm // tl, PEER_HEADS, nk, tl), F32)
    stat16 = pltpu.VMEM((tm // tl, PEER_HEADS, nk, tl), BF16)
    once = pl.Buffered(1)
    return pl.pallas_call(
        _peer_kernel,
        out_shape=jax.ShapeDtypeStruct((t, d), F32),
        grid=(t // tm, u.shape[0] // te),
        in_specs=[pl.BlockSpec((tm, qp.shape[1]), lambda i, e: (i, 0), pipeline_mode=once),
                  pl.BlockSpec(keys.shape, lambda i, e: (0, 0, 0), pipeline_mode=once),
                  pl.BlockSpec((tm, d), lambda i, e: (i, 0)),
                  pl.BlockSpec((te, d), lambda i, e: (e, 0)),
                  pl.BlockSpec((1, d, te), lambda i, e: (e, 0, 0)),
                  pl.BlockSpec((tm, d), lambda i, e: (i, 0), pipeline_mode=once),
                  pl.BlockSpec((1, 1, d), lambda i, e: (i // per_b, 0, 0)),
                  pl.BlockSpec((1, d), lambda i, e: (0, 0))],
        out_specs=pl.BlockSpec((tm, d), lambda i, e: (i, 0)),
        scratch_shapes=[stat, stat16, stat, stat16,
                        pltpu.VMEM((d, tm), F32),
                        pltpu.VMEM((2, PEER_TOPK, tl), F32),
                        pltpu.VMEM((80, tl), F32),
                        pltpu.VMEM((PEER_TOPK, tl), F32),
                        pltpu.VMEM((te, tm), BF16)],
        compiler_params=_params(("arbitrary", "arbitrary")),
        name="peer",
    )(qp, keys, h2, u, vt, x1, gate2, fin.reshape(1, d))


def kernel(x, c, w_ada, b_ada, norm_mix, norm_ffn, w_in, lb_logits, hgrn_gain, w_up_a, w_up_b,
           w_out, peer_w_q, peer_keys, peer_u, peer_v, final_norm):
    bsz, seq, d = x.shape
    depth = w_ada.shape[0]
    assert depth == 1, "the PEER kernel applies the final norm, so it must be the last layer"
    aw = A_HEADS * HEAD_DIM
    bw = B_HEADS * HEAD_DIM
    iw = IDX_HEADS * IDX_DIM
    lower_bounds = jnp.cumsum(jax.nn.softmax(lb_logits.astype(F32), axis=0), axis=0)
    xt = x.reshape(bsz * seq, d)

    widths = (aw, aw, aw, iw, IDX_DIM, IDX_HEADS, bw, bw, bw, bw, d, d)
    names = ("qa", "ka", "va", "qi", "ki", "wi", "qb", "fb", "ib", "gb", "g_a", "g_b")
    off, start = {}, 0
    for nme, wd in zip(names, widths):
        off[nme] = (start, start + wd)
        start += wd
    order1 = ("g_a", "g_b", "qa", "ka", "va", "qi", "qb", "ib", "gb")
    order2 = ("fb", "ki", "wi")
    pad2 = 128 - IDX_DIM - IDX_HEADS

    for l in range(depth):
        mod = _ada(c, w_ada[l], b_ada[l])
        shift1, scale1, gate1, shift2, scale2, gate2 = [
            m.reshape(bsz, 1, d) for m in jnp.split(mod, 6, axis=-1)]

        wl = w_in[l]
        w1 = jnp.concatenate([wl[:, off[n][0]:off[n][1]] for n in order1], axis=1).astype(BF16)
        w2 = jnp.concatenate([wl[:, off[n][0]:off[n][1]] for n in order2]
                             + [jnp.zeros((d, pad2), wl.dtype)], axis=1).astype(BF16)
        p1, p2 = _inproj(xt, norm_mix[l], scale1, shift1, w1, w2, seq, tm=1024, tn=1024)

        attn = _dsa(p1, p2, bsz, seq, {"qa": 4, "ka": 5, "va": 6, "qi": 7, "kw": 8})
        rec = _hgrn(p1, p2, lower_bounds[l], hgrn_gain[l], bsz, seq,
                    {"qb": 64, "ib": 72, "gb": 80, "fb": 0})

        x1, h2, qp = _mix(attn, rec, p1, xt, gate1,
                          w_up_a[l].astype(BF16), w_up_b[l].astype(BF16), w_out[l].astype(BF16),
                          norm_ffn[l], scale2, shift2, peer_w_q[l].astype(BF16),
                          seq, tm=256, col={"g_a": 0, "g_b": 1})
        keys = peer_keys[l].reshape(2 * PEER_HEADS, PEER_N_KEYS, -1).astype(BF16)
        te = 1024
        vt = peer_v[l].astype(BF16).reshape(-1, te, d).transpose(0, 2, 1)
        xt = _peer(qp, keys, h2, peer_u[l].astype(BF16), vt, x1, gate2, final_norm, seq, tm=512)
    return xt.reshape(bsz, seq, d)
```

```python
import functools

import jax
import jax.numpy as jnp
from jax import lax
from jax.experimental import pallas as pl
from jax.experimental.pallas import tpu as pltpu

F32 = jnp.float32
BF16 = jnp.bfloat16

NORM_EPS = 1e-6
CHUNK = 64
TOPK_MAX = 256
A_HEADS = 8
HEAD_DIM = 128
IDX_HEADS = 16
IDX_DIM = 64
B_HEADS = 8
PEER_HEADS = 8
PEER_N_KEYS = 128
PEER_TOPK = 16
SUB = 16

VMEM_LIMIT = 56 * 1024 * 1024
INT_MIN = -2 ** 31

NT_DIMS = (((1,), (1,)), ((), ()))
TN_DIMS = (((0,), (0,)), ((), ()))


def _sigmoid(x):
    return 1.0 / (1.0 + jnp.exp(-x))


def _params(sem):
    return pltpu.CompilerParams(dimension_semantics=sem, vmem_limit_bytes=VMEM_LIMIT)


def _ada_kernel(c_ref, w_ref, b_ref, o_ref):
    c = c_ref[...]
    cond = (c * _sigmoid(c)).astype(BF16)
    o_ref[...] = jnp.dot(cond, w_ref[...].astype(BF16), preferred_element_type=F32) + b_ref[...]


def _ada(c, w, b, tn=1024):
    bsz, d = c.shape
    n = w.shape[1]
    return pl.pallas_call(
        _ada_kernel,
        out_shape=jax.ShapeDtypeStruct((bsz, n), F32),
        grid=(n // tn,),
        in_specs=[pl.BlockSpec((bsz, d), lambda j: (0, 0)),
                  pl.BlockSpec((d, tn), lambda j: (0, j)),
                  pl.BlockSpec((1, tn), lambda j: (0, j))],
        out_specs=pl.BlockSpec((bsz, tn), lambda j: (0, j)),
        compiler_params=_params(("arbitrary",)),
        name="ada",
    )(c, w, b.reshape(1, n))


def _norm_mod(x, gain, scale, shift):
    ms = jnp.mean(x * x, axis=-1, keepdims=True)
    y = x * lax.rsqrt(ms + NORM_EPS) * gain
    return y * (1.0 + scale) + shift


def _inproj_kernel(x_ref, g_ref, sc_ref, sh_ref, w1_ref, w2_ref, o1_ref, o2_ref, h_ref):
    j = pl.program_id(1)
    n1 = pl.num_programs(1) - 1

    @pl.when(j == 0)
    def _():
        h_ref[...] = _norm_mod(x_ref[...], g_ref[...], sc_ref[0], sh_ref[0]).astype(BF16)

    @pl.when(j < n1)
    def _():
        o1_ref[...] = jnp.dot(h_ref[...], w1_ref[...],
                              preferred_element_type=F32).astype(o1_ref.dtype)

    @pl.when(j == n1)
    def _():
        o2_ref[...] = jnp.dot(h_ref[...], w2_ref[...], preferred_element_type=F32)


def _inproj(x2d, gain, scale, shift, w1, w2, seq, tm, tn):
    t, d = x2d.shape
    n1 = w1.shape[1] // tn
    n2 = w2.shape[1]
    per_b = seq // tm
    return pl.pallas_call(
        _inproj_kernel,
        out_shape=(jax.ShapeDtypeStruct((t, w1.shape[1]), BF16),
                   jax.ShapeDtypeStruct((t, n2), F32)),
        grid=(t // tm, n1 + 1),
        in_specs=[pl.BlockSpec((tm, d), lambda i, j: (i, 0)),
                  pl.BlockSpec((1, d), lambda i, j: (0, 0)),
                  pl.BlockSpec((1, 1, d), lambda i, j: (i // per_b, 0, 0)),
                  pl.BlockSpec((1, 1, d), lambda i, j: (i // per_b, 0, 0)),
                  pl.BlockSpec((d, tn), lambda i, j: (0, jnp.minimum(j, n1 - 1))),
                  pl.BlockSpec((d, n2), lambda i, j: (0, 0))],
        out_specs=(pl.BlockSpec((tm, tn), lambda i, j: (i, jnp.minimum(j, n1 - 1))),
                   pl.BlockSpec((tm, n2), lambda i, j: (i, 0))),
        scratch_shapes=[pltpu.VMEM((tm, d), BF16)],
        compiler_params=_params(("arbitrary", "arbitrary")),
        name="inproj",
    )(x2d, gain.reshape(1, d), scale, shift, w1, w2)


DSA_QB = 256
DSA_KT = 256
DSA_AT = 128
NEG_BIG = -1e30


def _dsa_kernel(qi_ref, kw_ref, qa_ref, ka_ref, va_ref, o_ref,
                key_ref, vt_ref, m_ref, l_ref, acc_ref):
    j = pl.program_id(1)
    qb, kt_rows = DSA_QB, DSA_KT
    seq = ka_ref.shape[0]
    n_kt = (j * qb + qb + kt_rows - 1) // kt_rows
    r0 = pl.multiple_of(j * qb, qb)

    @pl.when(j == 0)
    def _():
        for h in range(A_HEADS):
            for t in range(seq // kt_rows):
                blk = va_ref[t * kt_rows:(t + 1) * kt_rows, h * HEAD_DIM:(h + 1) * HEAD_DIM]
                vt_ref[h * HEAD_DIM:(h + 1) * HEAD_DIM, t * kt_rows:(t + 1) * kt_rows] = (
                    blk.astype(F32).T.astype(BF16))

    w_t = kw_ref[pl.ds(r0, qb), :].T
    q_chunk = (r0 + lax.broadcasted_iota(jnp.int32, (kt_rows, qb), 1)) // CHUNK
    key_iota = lax.broadcasted_iota(jnp.int32, (kt_rows, qb), 0)

    def score_tile(kt, carry):
        k0 = pl.multiple_of(kt * kt_rows, kt_rows)
        ki = kw_ref[pl.ds(k0, kt_rows), 0:IDX_DIM].astype(BF16)
        sc = jnp.zeros((kt_rows, qb), F32)
        for h in range(IDX_HEADS):
            dots = lax.dot_general(ki, qi_ref[:, h * IDX_DIM:(h + 1) * IDX_DIM], NT_DIMS,
                                   preferred_element_type=F32)
            sc = sc + w_t[IDX_DIM + h:IDX_DIM + h + 1, :] * jnp.maximum(dots, 0.0)
        sc = sc * ((IDX_DIM * IDX_HEADS) ** -0.5)
        bits = pltpu.bitcast(sc, jnp.int32)
        skey = bits ^ ((bits >> 31) & jnp.int32(0x7FFFFFFF))
        adm = (k0 + key_iota) // CHUNK <= q_chunk
        key_ref[pl.ds(k0, kt_rows), :] = jnp.where(adm, skey, jnp.int32(INT_MIN))
        return carry

    lax.fori_loop(0, n_kt, score_tile, 0)

    top_k = float(min(TOPK_MAX, seq // 4))

    def bit_step(it, v):
        cand = v + lax.shift_left(jnp.int32(1), 31 - it)

        def count_tile(kt, part):
            k0 = pl.multiple_of(kt * kt_rows, kt_rows)
            hit = jnp.where(key_ref[pl.ds(k0, kt_rows), :] >= cand, 1.0, 0.0)
            return part + jnp.sum(hit.reshape(kt_rows // 8, 8, qb), axis=0)

        part = lax.fori_loop(0, n_kt, count_tile, jnp.zeros((8, qb), F32))
        cnt = jnp.sum(part, axis=0, keepdims=True)
        return jnp.where(cnt >= top_k, cand, v)

    v = lax.fori_loop(0, 32, bit_step, jnp.full((1, qb), INT_MIN, jnp.int32))
    v = jnp.maximum(v, jnp.int32(INT_MIN + 1))

    att_scale = HEAD_DIM ** -0.5
    m_ref[...] = jnp.full(m_ref.shape, NEG_BIG, F32)
    l_ref[...] = jnp.zeros(l_ref.shape, F32)
    acc_ref[...] = jnp.zeros(acc_ref.shape, F32)

    at_rows = DSA_AT

    def att_tile(kt, carry):
        k0 = pl.multiple_of(kt * at_rows, at_rows)
        bias = jnp.where(key_ref[pl.ds(k0, at_rows), :] >= v, 0.0, NEG_BIG)
        for h in range(A_HEADS):
            cols = slice(h * HEAD_DIM, (h + 1) * HEAD_DIM)
            st = lax.dot_general(ka_ref[pl.ds(k0, at_rows), cols], qa_ref[:, cols], NT_DIMS,
                                 preferred_element_type=F32) * att_scale + bias
            m_old = m_ref[h:h + 1, :]
            m_new = jnp.maximum(m_old, jnp.max(st, axis=0, keepdims=True))
            alpha = jnp.exp(m_old - m_new)
            p = jnp.exp(st - m_new)
            m_ref[h:h + 1, :] = m_new
            l_ref[h:h + 1, :] = l_ref[h:h + 1, :] * alpha + jnp.sum(p, axis=0, keepdims=True)
            acc_ref[cols, :] = acc_ref[cols, :] * alpha + jnp.dot(
                vt_ref[cols, pl.ds(k0, at_rows)], p.astype(BF16), preferred_element_type=F32)
        return carry

    lax.fori_loop(0, n_kt * (kt_rows // at_rows), att_tile, 0)
    for h in range(A_HEADS):
        cols = slice(h * HEAD_DIM, (h + 1) * HEAD_DIM)
        o_ref[:, cols] = (acc_ref[cols, :] / l_ref[h:h + 1, :]).T.astype(o_ref.dtype)


def _dsa(p1, p2, bsz, seq, col):
    n_blk = seq // DSA_QB
    width = A_HEADS * HEAD_DIM
    return pl.pallas_call(
        _dsa_kernel,
        out_shape=jax.ShapeDtypeStruct((bsz * seq, width), BF16),
        grid=(bsz, n_blk),
        in_specs=[pl.BlockSpec((DSA_QB, width), lambda b, k: (b * n_blk + k, col["qi"])),
                  pl.BlockSpec((seq, 128), lambda b, k: (b, col["kw"])),
                  pl.BlockSpec((DSA_QB, width), lambda b, k: (b * n_blk + k, col["qa"])),
                  pl.BlockSpec((seq, width), lambda b, k: (b, col["ka"])),
                  pl.BlockSpec((seq, width), lambda b, k: (b, col["va"]))],
        out_specs=pl.BlockSpec((DSA_QB, width), lambda b, k: (b * n_blk + k, 0)),
        scratch_shapes=[pltpu.VMEM((seq, DSA_QB), jnp.int32), pltpu.VMEM((width, seq), BF16),
                        pltpu.VMEM((A_HEADS, DSA_QB), F32), pltpu.VMEM((A_HEADS, DSA_QB), F32),
                        pltpu.VMEM((width, DSA_QB), F32)],
        compiler_params=_params(("arbitrary", "arbitrary")),
        name="dsa",
    )(p1, p2, p1, p1, p1)


HGRN_GROUP = 4


def _hgrn_kernel(q_ref, f_ref, i_ref, g_ref, lb_ref, gain_ref, o_ref,
                 qe_ref, oi_ref, upd_ref, dec_ref):
    c = CHUNK
    n_c = q_ref.shape[0] // c
    d = q_ref.shape[1]
    lb = lb_ref[0]
    gain = gain_ref[...]
    chunk_row = lax.broadcasted_iota(jnp.int32, (c, d), 0)
    sub_row = lax.broadcasted_iota(jnp.int32, (SUB, d), 0)

    def local_part(ci):
        r0 = pl.multiple_of(ci * c, c)
        f = lb + (1.0 - lb) * _sigmoid(f_ref[pl.ds(r0, c), :])
        logf = jnp.log(f)
        k = 1.0 - f
        qr = q_ref[pl.ds(r0, c), :].astype(F32)
        q = qr * _sigmoid(qr)
        v = i_ref[pl.ds(r0, c), :].astype(F32)
        v16 = v.astype(BF16)
        b = logf
        shift = 1
        while shift < c:
            b = b + jnp.where(chunk_row >= shift, pltpu.roll(b, shift, 0), 0.0)
            shift *= 2
        b_last = b[c - 1:c, :]
        qe_ref[pl.ds(r0, c), :] = (q * jnp.exp(b)).astype(BF16)

        for blk in range(c // SUB):
            lo = blk * SUB
            q_i, k_i, v_i, b_i = q[lo:lo + SUB], k[lo:lo + SUB], v[lo:lo + SUB], b[lo:lo + SUB]
            o_i = jnp.zeros((SUB, d), F32)
            if blk > 0:
                b_ref = b[lo - 1:lo, :]
                qt = (q_i * jnp.exp(b_i - b_ref)).astype(BF16)
                kt = (k[:lo] * jnp.exp(b_ref - b[:lo])).astype(BF16)
                att = lax.dot_general(qt, kt, NT_DIMS, preferred_element_type=F32)
                o_i = jnp.dot(att.astype(BF16), v16[:lo], preferred_element_type=F32)
            for s_ in range(SUB):
                w = jnp.exp(jnp.where(sub_row >= s_, b_i - b_i[s_:s_ + 1, :], -jnp.inf))
                a = jnp.sum(q_i * k_i[s_:s_ + 1, :] * w, axis=1, keepdims=True)
                o_i = o_i + a * v_i[s_:s_ + 1, :]
            oi_ref[pl.ds(r0 + lo, SUB), :] = o_i

        k_end = (k * jnp.exp(b_last - b)).astype(BF16)
        upd_ref[ci] = lax.dot_general(v16, k_end, TN_DIMS, preferred_element_type=F32)
        dec_ref[ci] = jnp.exp(b_last)

    def group(gi, carry):
        for u in range(HGRN_GROUP):
            local_part(gi * HGRN_GROUP + u)
        return carry

    lax.fori_loop(0, n_c // HGRN_GROUP, group, 0)

    st_t = jnp.zeros((d, d), F32)
    for ci in range(n_c):
        rows = slice(ci * c, (ci + 1) * c)
        o = oi_ref[rows, :] + lax.dot_general(qe_ref[rows, :], st_t.astype(BF16), NT_DIMS,
                                              preferred_element_type=F32)
        ms = jnp.mean(o * o, axis=-1, keepdims=True)
        rec = o * lax.rsqrt(ms + NORM_EPS) * gain
        gr = g_ref[rows, :].astype(F32)
        o_ref[rows, :] = (rec * (gr * _sigmoid(gr))).astype(o_ref.dtype)
        st_t = st_t * dec_ref[ci] + upd_ref[ci]


def _hgrn(p1, p2, lb, gain, bsz, seq, col):
    d = HEAD_DIM
    n_c = seq // CHUNK
    return pl.pallas_call(
        _hgrn_kernel,
        out_shape=jax.ShapeDtypeStruct((bsz * seq, B_HEADS * d), BF16),
        grid=(bsz, B_HEADS),
        in_specs=[pl.BlockSpec((seq, d), lambda b, h: (b, col["qb"] + h)),
                  pl.BlockSpec((seq, d), lambda b, h: (b, col["fb"] + h)),
                  pl.BlockSpec((seq, d), lambda b, h: (b, col["ib"] + h)),
                  pl.BlockSpec((seq, d), lambda b, h: (b, col["gb"] + h)),
                  pl.BlockSpec((1, 1, d), lambda b, h: (h, 0, 0)),
                  pl.BlockSpec((1, d), lambda b, h: (0, 0))],
        out_specs=pl.BlockSpec((seq, d), lambda b, h: (b, h)),
        scratch_shapes=[pltpu.VMEM((seq, d), BF16), pltpu.VMEM((seq, d), F32),
                        pltpu.VMEM((n_c, d, d), F32), pltpu.VMEM((n_c, 1, d), F32)],
        compiler_params=_params(("arbitrary", "arbitrary")),
        name="hgrn",
    )(p1, p2, p1, p1, lb.reshape(B_HEADS, 1, d), gain.reshape(1, d))


def _mix_kernel(attn_ref, rec_ref, ga_ref, gb_ref, x_ref, gate_ref, wa_ref, wb_ref, wo_ref,
                g2_ref, sc_ref, sh_ref, wq_ref, x1_ref, h2_ref, qp_ref):
    ya = jnp.dot(attn_ref[...], wa_ref[...], preferred_element_type=F32)
    yb = jnp.dot(rec_ref[...], wb_ref[...], preferred_element_type=F32)
    mixed = _sigmoid(ga_ref[...].astype(F32)) * ya + _sigmoid(gb_ref[...].astype(F32)) * yb
    x1 = x_ref[...] + gate_ref[0] * jnp.dot(mixed.astype(BF16), wo_ref[...],
                                            preferred_element_type=F32)
    x1_ref[...] = x1
    h2 = _norm_mod(x1, g2_ref[...], sc_ref[0], sh_ref[0]).astype(BF16)
    h2_ref[...] = h2
    qp_ref[...] = jnp.dot(h2, wq_ref[...], preferred_element_type=F32).astype(qp_ref.dtype)


def _mix(attn, rec, p1, x2d, gate1, wa, wb, wo, g2, scale2, shift2, wq, seq, tm, col):
    t, d = x2d.shape
    wa_w = attn.shape[1]
    per_b = seq // tm
    full = lambda shape: pl.BlockSpec(shape, lambda i: (0,) * len(shape),
                                      pipeline_mode=pl.Buffered(1))
    mod = pl.BlockSpec((1, 1, d), lambda i: (i // per_b, 0, 0))
    return pl.pallas_call(
        _mix_kernel,
        out_shape=(jax.ShapeDtypeStruct((t, d), F32), jax.ShapeDtypeStruct((t, d), BF16),
                   jax.ShapeDtypeStruct((t, wq.shape[1]), BF16)),
        grid=(t // tm,),
        in_specs=[pl.BlockSpec((tm, wa_w), lambda i: (i, 0)),
                  pl.BlockSpec((tm, wa_w), lambda i: (i, 0)),
                  pl.BlockSpec((tm, d), lambda i: (i, col["g_a"])),
                  pl.BlockSpec((tm, d), lambda i: (i, col["g_b"])),
                  pl.BlockSpec((tm, d), lambda i: (i, 0)),
                  mod, full(wa.shape), full(wb.shape), full(wo.shape),
                  full((1, d)), mod, mod, full(wq.shape)],
        out_specs=(pl.BlockSpec((tm, d), lambda i: (i, 0)),
                   pl.BlockSpec((tm, d), lambda i: (i, 0)),
                   pl.BlockSpec((tm, wq.shape[1]), lambda i: (i, 0))),
        compiler_params=_params(("arbitrary",)),
        name="mix",
    )(attn, rec, p1, p1, x2d, gate1, wa, wb, wo, g2.reshape(1, d), scale2, shift2, wq)


def _top_values(work, n, dst_ref):
    rank = jnp.full(work.shape, float(n), F32)
    for r in range(n):
        mx = jnp.max(work, axis=0, keepdims=True)
        dst_ref[r:r + 1, :] = mx
        hit = work == mx
        rank = jnp.where(hit, float(r), rank)
        work = jnp.where(hit, -jnp.inf, work)
    return rank


def _gelu(x):
    return 0.5 * x * (1.0 + lax.erf(x * (2.0 ** -0.5)))


PEER_LANES = 128


def _peer_stats(qp_ref, keys_ref, m_ref, rank_ref, c0_ref, e1_ref, top_ref, cand_ref, best_ref):
    nk, kk, tl = PEER_N_KEYS, PEER_TOPK, PEER_LANES
    row8 = lax.broadcasted_iota(jnp.int32, (8, tl), 0)

    def chunk(c, carry):
        t0 = pl.multiple_of(c * tl, tl)
        for p in range(PEER_HEADS):
            scores = []
            for half in range(2):
                j = 2 * p + half
                qs = qp_ref[pl.ds(t0, tl), j * nk:(j + 1) * nk]
                scores.append(lax.dot_general(keys_ref[j], qs, NT_DIMS,
                                              preferred_element_type=F32))
            s0, s1 = scores
            _top_values(s0, kk, top_ref.at[0])
            rank_ref[c, p] = _top_values(s1, kk, top_ref.at[1]).astype(BF16)
            cand_ref[0:kk, :] = top_ref[0, 0:1, :] + top_ref[1]
            cand_ref[kk:kk + 8, :] = top_ref[0, 1:2, :] + top_ref[1, 0:8, :]
            for a in range(2, 8):
                grp = top_ref[0, a:a + 1, :] + top_ref[1, 0:8, :]
                cand_ref[8 + 8 * a:16 + 8 * a, :] = jnp.where(row8 < kk // (a + 1), grp, -jnp.inf)
            cand_ref[72:80, :] = top_ref[0, 8:16, :] + top_ref[1, 0:1, :]
            _top_values(cand_ref[...], kk, best_ref)
            best = best_ref[...]
            z = jnp.sum(jnp.exp(best - best[0:1, :]), axis=0, keepdims=True)
            thr = best[kk - 1:kk, :]
            m = jnp.zeros(s0.shape, F32)
            for b in range(4):
                m = m + jnp.where(s0 + top_ref[1, b:b + 1, :] >= thr, 1.0, 0.0)
            for a in range(3):
                tail = top_ref[0, a:a + 1, :] + top_ref[1, 4:kk // (a + 1), :]
                extra = jnp.sum(jnp.where(tail >= thr, 1.0, 0.0), axis=0, keepdims=True)
                m = m + jnp.where(s0 == top_ref[0, a:a + 1, :], extra, 0.0)
            m_ref[c, p] = m
            c0_ref[c, p] = jnp.exp(s0 - top_ref[0, 0:1, :]) / z
            e1_ref[c, p] = jnp.exp(s1 - top_ref[1, 0:1, :]).astype(BF16)
        return carry

    lax.fori_loop(0, m_ref.shape[0], chunk, 0)


def _peer_kernel(qp_ref, keys_ref, h2_ref, u_ref, vt_ref, x1_ref, gate_ref, fin_ref, o_ref,
                 m_ref, rank_ref, c0_ref, e1_ref, acc_ref, top_ref, cand_ref, best_ref, act_ref):
    eb = pl.program_id(1)
    n_eb = pl.num_programs(1)
    nk, tl = PEER_N_KEYS, PEER_LANES
    rows = u_ref.shape[0] // nk

    @pl.when(eb == 0)
    def _():
        acc_ref[...] = jnp.zeros_like(acc_ref)
        _peer_stats(qp_ref, keys_ref, m_ref, rank_ref, c0_ref, e1_ref, top_ref, cand_ref, best_ref)

    a_t = lax.dot_general(u_ref[...], h2_ref[...], NT_DIMS, preferred_element_type=F32)
    for r in range(rows):
        i = eb * rows + r
        sl = slice(r * nk, (r + 1) * nk)
        for c in range(m_ref.shape[0]):
            lanes = slice(c * tl, (c + 1) * tl)
            gmat = jnp.zeros((nk, tl), BF16)
            for p in range(PEER_HEADS):
                c0row = jnp.broadcast_to(c0_ref[c, p, pl.ds(i, 1), :].astype(BF16), (nk, tl))
                mrow = jnp.broadcast_to(m_ref[c, p, pl.ds(i, 1), :].astype(BF16), (nk, tl))
                val = c0row * e1_ref[c, p]
                gmat = gmat + jnp.where(rank_ref[c, p] < mrow, val, jnp.zeros_like(val))
            act_ref[sl, lanes] = _gelu(a_t[sl, lanes]).astype(BF16) * gmat
    acc_ref[...] += jnp.dot(vt_ref[0], act_ref[...], preferred_element_type=F32)

    @pl.when(eb == n_eb - 1)
    def _():
        x2 = x1_ref[...] + gate_ref[0] * acc_ref[...].T
        ms = jnp.mean(x2 * x2, axis=-1, keepdims=True)
        o_ref[...] = x2 * lax.rsqrt(ms + NORM_EPS) * fin_ref[...]


def _peer(qp, keys, h2, u, vt, x1, gate2, fin, seq, tm):
    t, d = x1.shape
    te = vt.shape[2]
    per_b = seq // tm
    nk, tl = PEER_N_KEYS, PEER_LANES
    stat = pltpu.VMEM((tm // tl, PEER_HEADS, nk, tl), F32)
    stat16 = pltpu.VMEM((tm // tl, PEER_HEADS, nk, tl), BF16)
    once = pl.Buffered(1)
    return pl.pallas_call(
        _peer_kernel,
        out_shape=jax.ShapeDtypeStruct((t, d), F32),
        grid=(t // tm, u.shape[0] // te),
        in_specs=[pl.BlockSpec((tm, qp.shape[1]), lambda i, e: (i, 0), pipeline_mode=once),
                  pl.BlockSpec(keys.shape, lambda i, e: (0, 0, 0), pipeline_mode=once),
                  pl.BlockSpec((tm, d), lambda i, e: (i, 0)),
                  pl.BlockSpec((te, d), lambda i, e: (e, 0)),
                  pl.BlockSpec((1, d, te), lambda i, e: (e, 0, 0)),
                  pl.BlockSpec((tm, d), lambda i, e: (i, 0), pipeline_mode=once),
                  pl.BlockSpec((1, 1, d), lambda i, e: (i // per_b, 0, 0)),
                  pl.BlockSpec((1, d), lambda i, e: (0, 0))],
        out_specs=pl.BlockSpec((tm, d), lambda i, e: (i, 0)),
        scratch_shapes=[stat, stat16, stat, stat16,
                        pltpu.VMEM((d, tm), F32),
                        pltpu.VMEM((2, PEER_TOPK, tl), F32),
                        pltpu.VMEM((80, tl), F32),
                        pltpu.VMEM((PEER_TOPK, tl), F32),
                        pltpu.VMEM((te, tm), BF16)],
        compiler_params=_params(("arbitrary", "arbitrary")),
        name="peer",
    )(qp, keys, h2, u, vt, x1, gate2, fin.reshape(1, d))


def kernel(x, c, w_ada, b_ada, norm_mix, norm_ffn, w_in, lb_logits, hgrn_gain, w_up_a, w_up_b,
           w_out, peer_w_q, peer_keys, peer_u, peer_v, final_norm):
    bsz, seq, d = x.shape
    depth = w_ada.shape[0]
    assert depth == 1, "the PEER kernel applies the final norm, so it must be the last layer"
    aw = A_HEADS * HEAD_DIM
    bw = B_HEADS * HEAD_DIM
    iw = IDX_HEADS * IDX_DIM
    lower_bounds = jnp.cumsum(jax.nn.softmax(lb_logits.astype(F32), axis=0), axis=0)
    xt = x.reshape(bsz * seq, d)

    widths = (aw, aw, aw, iw, IDX_DIM, IDX_HEADS, bw, bw, bw, bw, d, d)
    names = ("qa", "ka", "va", "qi", "ki", "wi", "qb", "fb", "ib", "gb", "g_a", "g_b")
    off, start = {}, 0
    for nme, wd in zip(names, widths):
        off[nme] = (start, start + wd)
        start += wd
    order1 = ("g_a", "g_b", "qa", "ka", "va", "qi", "qb", "ib", "gb")
    order2 = ("fb", "ki", "wi")
    pad2 = 128 - IDX_DIM - IDX_HEADS

    for l in range(depth):
        mod = _ada(c, w_ada[l], b_ada[l])
        shift1, scale1, gate1, shift2, scale2, gate2 = [
            m.reshape(bsz, 1, d) for m in jnp.split(mod, 6, axis=-1)]

        wl = w_in[l].astype(BF16)
        w1 = jnp.concatenate([wl[:, off[n][0]:off[n][1]] for n in order1], axis=1)
        w2 = jnp.concatenate([wl[:, off[n][0]:off[n][1]] for n in order2]
                             + [jnp.zeros((d, pad2), BF16)], axis=1)
        p1, p2 = _inproj(xt, norm_mix[l], scale1, shift1, w1, w2, seq, tm=1024, tn=1024)

        attn = _dsa(p1, p2, bsz, seq, {"qa": 4, "ka": 5, "va": 6, "qi": 7, "kw": 8})
        rec = _hgrn(p1, p2, lower_bounds[l], hgrn_gain[l], bsz, seq,
                    {"qb": 64, "ib": 72, "gb": 80, "fb": 0})

        x1, h2, qp = _mix(attn, rec, p1, xt, gate1,
                          w_up_a[l].astype(BF16), w_up_b[l].astype(BF16), w_out[l].astype(BF16),
                          norm_ffn[l], scale2, shift2, peer_w_q[l].astype(BF16),
                          seq, tm=256, col={"g_a": 0, "g_b": 1})
        keys = peer_keys[l].reshape(2 * PEER_HEADS, PEER_N_KEYS, -1).astype(BF16)
        te = 1024
        vt = peer_v[l].astype(BF16).reshape(-1, te, d).transpose(0, 2, 1)
        xt = _peer(qp, keys, h2, peer_u[l].astype(BF16), vt, x1, gate2, final_norm, seq, tm=512)
    return xt.reshape(bsz, seq, d)
```

```python
import jax
import jax.numpy as jnp
from jax import lax
from jax.experimental import pallas as pl
from jax.experimental.pallas import tpu as pltpu

F32 = jnp.float32
BF16 = jnp.bfloat16

NORM_EPS = 1e-6
CHUNK = 64
TOPK_MAX = 256
A_HEADS = 8
HEAD_DIM = 128
IDX_HEADS = 16
IDX_DIM = 64
B_HEADS = 8
PEER_HEADS = 8
PEER_N_KEYS = 128
PEER_TOPK = 16
SUB = 16

LANES = 128
VMEM_LIMIT = 56 * 1024 * 1024
INT_MIN = -2 ** 31

TILES = dict(
    ada_tn=1024,
    inproj_tm=1024, inproj_tn=1024,
    mix_tm=256,
    peer_tm=512,
    peer_te=1024,
)

NT_DIMS = (((1,), (1,)), ((), ()))
TN_DIMS = (((0,), (0,)), ((), ()))


def _sigmoid(x):
    return 1.0 / (1.0 + jnp.exp(-x))


def _params(sem):
    return pltpu.CompilerParams(dimension_semantics=sem, vmem_limit_bytes=VMEM_LIMIT)


def _ada_kernel(c_ref, w_ref, b_ref, o_ref):
    c = c_ref[...]
    cond = (c * _sigmoid(c)).astype(BF16)
    o_ref[...] = jnp.dot(cond, w_ref[...].astype(BF16), preferred_element_type=F32) + b_ref[...]


def _ada(c, w, b, tn):
    bsz, d = c.shape
    n = w.shape[1]
    return pl.pallas_call(
        _ada_kernel,
        out_shape=jax.ShapeDtypeStruct((bsz, n), F32),
        grid=(n // tn,),
        in_specs=[pl.BlockSpec((bsz, d), lambda j: (0, 0)),
                  pl.BlockSpec((d, tn), lambda j: (0, j)),
                  pl.BlockSpec((1, tn), lambda j: (0, j))],
        out_specs=pl.BlockSpec((bsz, tn), lambda j: (0, j)),
        compiler_params=_params(("arbitrary",)),
        name="ada",
    )(c, w, b.reshape(1, n))


def _norm_mod(x, gain, scale, shift):
    ms = jnp.mean(x * x, axis=-1, keepdims=True)
    y = x * lax.rsqrt(ms + NORM_EPS) * gain
    return y * (1.0 + scale) + shift


def _inproj_kernel(x_ref, g_ref, sc_ref, sh_ref, w1_ref, w2_ref, o1_ref, o2_ref, h_ref):
    j = pl.program_id(1)
    n1 = pl.num_programs(1) - 1

    @pl.when(j == 0)
    def _():
        h_ref[...] = _norm_mod(x_ref[...], g_ref[...], sc_ref[0], sh_ref[0]).astype(BF16)

    @pl.when(j < n1)
    def _():
        o1_ref[...] = jnp.dot(h_ref[...], w1_ref[...],
                              preferred_element_type=F32).astype(o1_ref.dtype)

    @pl.when(j == n1)
    def _():
        o2_ref[...] = jnp.dot(h_ref[...], w2_ref[...], preferred_element_type=F32)


def _inproj(x2d, gain, scale, shift, w1, w2, seq, tm, tn):
    t, d = x2d.shape
    n1 = w1.shape[1] // tn
    n2 = w2.shape[1]
    per_b = seq // tm
    return pl.pallas_call(
        _inproj_kernel,
        out_shape=(jax.ShapeDtypeStruct((t, w1.shape[1]), BF16),
                   jax.ShapeDtypeStruct((t, n2), F32)),
        grid=(t // tm, n1 + 1),
        in_specs=[pl.BlockSpec((tm, d), lambda i, j: (i, 0)),
                  pl.BlockSpec((1, d), lambda i, j: (0, 0)),
                  pl.BlockSpec((1, 1, d), lambda i, j: (i // per_b, 0, 0)),
                  pl.BlockSpec((1, 1, d), lambda i, j: (i // per_b, 0, 0)),
                  pl.BlockSpec((d, tn), lambda i, j: (0, jnp.minimum(j, n1 - 1))),
                  pl.BlockSpec((d, n2), lambda i, j: (0, 0))],
        out_specs=(pl.BlockSpec((tm, tn), lambda i, j: (i, jnp.minimum(j, n1 - 1))),
                   pl.BlockSpec((tm, n2), lambda i, j: (i, 0))),
        scratch_shapes=[pltpu.VMEM((tm, d), BF16)],
        compiler_params=_params(("arbitrary", "arbitrary")),
        name="inproj",
    )(x2d, gain.reshape(1, d), scale, shift, w1, w2)


DSA_QB = 256
DSA_KT = 256
DSA_AT = LANES
NEG_BIG = -1e30


def _dsa_kernel(qi_ref, kw_ref, qa_ref, ka_ref, va_ref, o_ref,
                key_ref, vt_ref, m_ref, l_ref, acc_ref):
    j = pl.program_id(1)
    qb, kt_rows = DSA_QB, DSA_KT
    seq = ka_ref.shape[0]
    n_kt = (j * qb + qb + kt_rows - 1) // kt_rows
    r0 = pl.multiple_of(j * qb, qb)

    @pl.when(j == 0)
    def _():
        for h in range(A_HEADS):
            for t in range(seq // kt_rows):
                blk = va_ref[t * kt_rows:(t + 1) * kt_rows, h * HEAD_DIM:(h + 1) * HEAD_DIM]
                vt_ref[h * HEAD_DIM:(h + 1) * HEAD_DIM, t * kt_rows:(t + 1) * kt_rows] = (
                    blk.astype(F32).T.astype(BF16))

    w_t = kw_ref[pl.ds(r0, qb), :].T
    q_chunk = (r0 + lax.broadcasted_iota(jnp.int32, (kt_rows, qb), 1)) // CHUNK
    key_iota = lax.broadcasted_iota(jnp.int32, (kt_rows, qb), 0)

    def score_tile(kt, carry):
        k0 = pl.multiple_of(kt * kt_rows, kt_rows)
        ki = kw_ref[pl.ds(k0, kt_rows), 0:IDX_DIM].astype(BF16)
        sc = jnp.zeros((kt_rows, qb), F32)
        for h in range(IDX_HEADS):
            dots = lax.dot_general(ki, qi_ref[:, h * IDX_DIM:(h + 1) * IDX_DIM], NT_DIMS,
                                   preferred_element_type=F32)
            sc = sc + w_t[IDX_DIM + h:IDX_DIM + h + 1, :] * jnp.maximum(dots, 0.0)
        sc = sc * ((IDX_DIM * IDX_HEADS) ** -0.5)
        bits = pltpu.bitcast(sc, jnp.int32)
        skey = bits ^ ((bits >> 31) & jnp.int32(0x7FFFFFFF))
        adm = (k0 + key_iota) // CHUNK <= q_chunk
        key_ref[pl.ds(k0, kt_rows), :] = jnp.where(adm, skey, jnp.int32(INT_MIN))
        return carry

    lax.fori_loop(0, n_kt, score_tile, 0)

    top_k = float(min(TOPK_MAX, seq // 4))

    def bit_step(it, v):
        cand = v + lax.shift_left(jnp.int32(1), 31 - it)

        def count_tile(kt, part):
            k0 = pl.multiple_of(kt * kt_rows, kt_rows)
            hit = jnp.where(key_ref[pl.ds(k0, kt_rows), :] >= cand, 1.0, 0.0)
            return part + jnp.sum(hit.reshape(kt_rows // 8, 8, qb), axis=0)

        part = lax.fori_loop(0, n_kt, count_tile, jnp.zeros((8, qb), F32))
        cnt = jnp.sum(part, axis=0, keepdims=True)
        return jnp.where(cnt >= top_k, cand, v)

    v = lax.fori_loop(0, 32, bit_step, jnp.full((1, qb), INT_MIN, jnp.int32))
    v = jnp.maximum(v, jnp.int32(INT_MIN + 1))

    att_scale = HEAD_DIM ** -0.5
    m_ref[...] = jnp.full(m_ref.shape, NEG_BIG, F32)
    l_ref[...] = jnp.zeros(l_ref.shape, F32)
    acc_ref[...] = jnp.zeros(acc_ref.shape, F32)

    at_rows = DSA_AT

    def att_tile(kt, carry):
        k0 = pl.multiple_of(kt * at_rows, at_rows)
        bias = jnp.where(key_ref[pl.ds(k0, at_rows), :] >= v, 0.0, NEG_BIG)
        for h in range(A_HEADS):
            cols = slice(h * HEAD_DIM, (h + 1) * HEAD_DIM)
            st = lax.dot_general(ka_ref[pl.ds(k0, at_rows), cols], qa_ref[:, cols], NT_DIMS,
                                 preferred_element_type=F32) * att_scale + bias
            m_old = m_ref[h:h + 1, :]
            m_new = jnp.maximum(m_old, jnp.max(st, axis=0, keepdims=True))
            alpha = jnp.exp(m_old - m_new)
            p = jnp.exp(st - m_new)
            m_ref[h:h + 1, :] = m_new
            l_ref[h:h + 1, :] = l_ref[h:h + 1, :] * alpha + jnp.sum(p, axis=0, keepdims=True)
            acc_ref[cols, :] = acc_ref[cols, :] * alpha + jnp.dot(
                vt_ref[cols, pl.ds(k0, at_rows)], p.astype(BF16), preferred_element_type=F32)
        return carry

    lax.fori_loop(0, n_kt * (kt_rows // at_rows), att_tile, 0)
    for h in range(A_HEADS):
        cols = slice(h * HEAD_DIM, (h + 1) * HEAD_DIM)
        o_ref[:, cols] = (acc_ref[cols, :] / l_ref[h:h + 1, :]).T.astype(o_ref.dtype)


def _dsa(p1, p2, bsz, seq, col):
    n_blk = seq // DSA_QB
    width = A_HEADS * HEAD_DIM
    return pl.pallas_call(
        _dsa_kernel,
        out_shape=jax.ShapeDtypeStruct((bsz * seq, width), BF16),
        grid=(bsz, n_blk),
        in_specs=[pl.BlockSpec((DSA_QB, width), lambda b, k: (b * n_blk + k, col["qi"])),
                  pl.BlockSpec((seq, LANES), lambda b, k: (b, col["kw"])),
                  pl.BlockSpec((DSA_QB, width), lambda b, k: (b * n_blk + k, col["qa"])),
                  pl.BlockSpec((seq, width), lambda b, k: (b, col["ka"])),
                  pl.BlockSpec((seq, width), lambda b, k: (b, col["va"]))],
        out_specs=pl.BlockSpec((DSA_QB, width), lambda b, k: (b * n_blk + k, 0)),
        scratch_shapes=[pltpu.VMEM((seq, DSA_QB), jnp.int32), pltpu.VMEM((width, seq), BF16),
                        pltpu.VMEM((A_HEADS, DSA_QB), F32), pltpu.VMEM((A_HEADS, DSA_QB), F32),
                        pltpu.VMEM((width, DSA_QB), F32)],
        compiler_params=_params(("arbitrary", "arbitrary")),
        name="dsa",
    )(p1, p2, p1, p1, p1)


HGRN_GROUP = 4


def _hgrn_kernel(q_ref, f_ref, i_ref, g_ref, lb_ref, gain_ref, o_ref,
                 qe_ref, oi_ref, upd_ref, dec_ref):
    c = CHUNK
    n_c = q_ref.shape[0] // c
    d = q_ref.shape[1]
    lb = lb_ref[0]
    gain = gain_ref[...]
    chunk_row = lax.broadcasted_iota(jnp.int32, (c, d), 0)
    sub_row = lax.broadcasted_iota(jnp.int32, (SUB, d), 0)

    def local_part(ci):
        r0 = pl.multiple_of(ci * c, c)
        f = lb + (1.0 - lb) * _sigmoid(f_ref[pl.ds(r0, c), :])
        logf = jnp.log(f)
        k = 1.0 - f
        qr = q_ref[pl.ds(r0, c), :].astype(F32)
        q = qr * _sigmoid(qr)
        v = i_ref[pl.ds(r0, c), :].astype(F32)
        v16 = v.astype(BF16)
        b = logf
        shift = 1
        while shift < c:
            b = b + jnp.where(chunk_row >= shift, pltpu.roll(b, shift, 0), 0.0)
            shift *= 2
        b_last = b[c - 1:c, :]
        qe_ref[pl.ds(r0, c), :] = (q * jnp.exp(b)).astype(BF16)

        for blk in range(c // SUB):
            lo = blk * SUB
            q_i, k_i, v_i, b_i = q[lo:lo + SUB], k[lo:lo + SUB], v[lo:lo + SUB], b[lo:lo + SUB]
            o_i = jnp.zeros((SUB, d), F32)
            if blk > 0:
                b_ref = b[lo - 1:lo, :]
                qt = (q_i * jnp.exp(b_i - b_ref)).astype(BF16)
                kt = (k[:lo] * jnp.exp(b_ref - b[:lo])).astype(BF16)
                att = lax.dot_general(qt, kt, NT_DIMS, preferred_element_type=F32)
                o_i = jnp.dot(att.astype(BF16), v16[:lo], preferred_element_type=F32)
            for s_ in range(SUB):
                w = jnp.exp(jnp.where(sub_row >= s_, b_i - b_i[s_:s_ + 1, :], -jnp.inf))
                a = jnp.sum(q_i * k_i[s_:s_ + 1, :] * w, axis=1, keepdims=True)
                o_i = o_i + a * v_i[s_:s_ + 1, :]
            oi_ref[pl.ds(r0 + lo, SUB), :] = o_i

        k_end = (k * jnp.exp(b_last - b)).astype(BF16)
        upd_ref[ci] = lax.dot_general(v16, k_end, TN_DIMS, preferred_element_type=F32)
        dec_ref[ci] = jnp.exp(b_last)

    def group(gi, carry):
        for u in range(HGRN_GROUP):
            local_part(gi * HGRN_GROUP + u)
        return carry

    lax.fori_loop(0, n_c // HGRN_GROUP, group, 0)

    st_t = jnp.zeros((d, d), F32)
    for ci in range(n_c):
        rows = slice(ci * c, (ci + 1) * c)
        o = oi_ref[rows, :] + lax.dot_general(qe_ref[rows, :], st_t.astype(BF16), NT_DIMS,
                                              preferred_element_type=F32)
        ms = jnp.mean(o * o, axis=-1, keepdims=True)
        rec = o * lax.rsqrt(ms + NORM_EPS) * gain
        gr = g_ref[rows, :].astype(F32)
        o_ref[rows, :] = (rec * (gr * _sigmoid(gr))).astype(o_ref.dtype)
        st_t = st_t * dec_ref[ci] + upd_ref[ci]


def _hgrn(p1, p2, lb, gain, bsz, seq, col):
    d = HEAD_DIM
    n_c = seq // CHUNK
    return pl.pallas_call(
        _hgrn_kernel,
        out_shape=jax.ShapeDtypeStruct((bsz * seq, B_HEADS * d), BF16),
        grid=(bsz, B_HEADS),
        in_specs=[pl.BlockSpec((seq, d), lambda b, h: (b, col["qb"] + h)),
                  pl.BlockSpec((seq, d), lambda b, h: (b, col["fb"] + h)),
                  pl.BlockSpec((seq, d), lambda b, h: (b, col["ib"] + h)),
                  pl.BlockSpec((seq, d), lambda b, h: (b, col["gb"] + h)),
                  pl.BlockSpec((1, 1, d), lambda b, h: (h, 0, 0)),
                  pl.BlockSpec((1, d), lambda b, h: (0, 0))],
        out_specs=pl.BlockSpec((seq, d), lambda b, h: (b, h)),
        scratch_shapes=[pltpu.VMEM((seq, d), BF16), pltpu.VMEM((seq, d), F32),
                        pltpu.VMEM((n_c, d, d), F32), pltpu.VMEM((n_c, 1, d), F32)],
        compiler_params=_params(("arbitrary", "arbitrary")),
        name="hgrn",
    )(p1, p2, p1, p1, lb.reshape(B_HEADS, 1, d), gain.reshape(1, d))


def _mix_kernel(attn_ref, rec_ref, ga_ref, gb_ref, x_ref, gate_ref, wa_ref, wb_ref, wo_ref,
                g2_ref, sc_ref, sh_ref, wq_ref, x1_ref, h2_ref, qp_ref):
    ya = jnp.dot(attn_ref[...], wa_ref[...], preferred_element_type=F32)
    yb = jnp.dot(rec_ref[...], wb_ref[...], preferred_element_type=F32)
    mixed = _sigmoid(ga_ref[...].astype(F32)) * ya + _sigmoid(gb_ref[...].astype(F32)) * yb
    x1 = x_ref[...] + gate_ref[0] * jnp.dot(mixed.astype(BF16), wo_ref[...],
                                            preferred_element_type=F32)
    x1_ref[...] = x1
    h2 = _norm_mod(x1, g2_ref[...], sc_ref[0], sh_ref[0]).astype(BF16)
    h2_ref[...] = h2
    qp_ref[...] = jnp.dot(h2, wq_ref[...], preferred_element_type=F32).astype(qp_ref.dtype)


def _mix(attn, rec, p1, x2d, gate1, wa, wb, wo, g2, scale2, shift2, wq, seq, tm, col):
    t, d = x2d.shape
    wa_w = attn.shape[1]
    per_b = seq // tm
    full = lambda shape: pl.BlockSpec(shape, lambda i: (0,) * len(shape),
                                      pipeline_mode=pl.Buffered(1))
    mod = pl.BlockSpec((1, 1, d), lambda i: (i // per_b, 0, 0))
    return pl.pallas_call(
        _mix_kernel,
        out_shape=(jax.ShapeDtypeStruct((t, d), F32), jax.ShapeDtypeStruct((t, d), BF16),
                   jax.ShapeDtypeStruct((t, wq.shape[1]), BF16)),
        grid=(t // tm,),
        in_specs=[pl.BlockSpec((tm, wa_w), lambda i: (i, 0)),
                  pl.BlockSpec((tm, wa_w), lambda i: (i, 0)),
                  pl.BlockSpec((tm, d), lambda i: (i, col["g_a"])),
                  pl.BlockSpec((tm, d), lambda i: (i, col["g_b"])),
                  pl.BlockSpec((tm, d), lambda i: (i, 0)),
                  mod, full(wa.shape), full(wb.shape), full(wo.shape),
                  full((1, d)), mod, mod, full(wq.shape)],
        out_specs=(pl.BlockSpec((tm, d), lambda i: (i, 0)),
                   pl.BlockSpec((tm, d), lambda i: (i, 0)),
                   pl.BlockSpec((tm, wq.shape[1]), lambda i: (i, 0))),
        compiler_params=_params(("arbitrary",)),
        name="mix",
    )(attn, rec, p1, p1, x2d, gate1, wa, wb, wo, g2.reshape(1, d), scale2, shift2, wq)


def _top_values(work, n, dst_ref):
    rank = jnp.full(work.shape, float(n), F32)
    for r in range(n):
        mx = jnp.max(work, axis=0, keepdims=True)
        dst_ref[r:r + 1, :] = mx
        hit = work == mx
        rank = jnp.where(hit, float(r), rank)
        work = jnp.where(hit, -jnp.inf, work)
    return rank


def _gelu(x):
    return 0.5 * x * (1.0 + lax.erf(x * (2.0 ** -0.5)))


PEER_CAND_ROWS = 5 * PEER_TOPK
PEER_LANES = LANES


def _peer_stats(qp_ref, keys_ref, m_ref, rank_ref, c0_ref, e1_ref, top_ref, cand_ref, best_ref):
    nk, kk, tl = PEER_N_KEYS, PEER_TOPK, PEER_LANES
    row8 = lax.broadcasted_iota(jnp.int32, (8, tl), 0)

    def chunk(c, carry):
        t0 = pl.multiple_of(c * tl, tl)
        for p in range(PEER_HEADS):
            scores = []
            for half in range(2):
                j = 2 * p + half
                qs = qp_ref[pl.ds(t0, tl), j * nk:(j + 1) * nk]
                scores.append(lax.dot_general(keys_ref[j], qs, NT_DIMS,
                                              preferred_element_type=F32))
            s0, s1 = scores
            _top_values(s0, kk, top_ref.at[0])
            rank_ref[c, p] = _top_values(s1, kk, top_ref.at[1]).astype(BF16)
            cand_ref[0:kk, :] = top_ref[0, 0:1, :] + top_ref[1]
            cand_ref[kk:kk + 8, :] = top_ref[0, 1:2, :] + top_ref[1, 0:8, :]
            for a in range(2, 8):
                grp = top_ref[0, a:a + 1, :] + top_ref[1, 0:8, :]
                cand_ref[8 + 8 * a:16 + 8 * a, :] = jnp.where(row8 < kk // (a + 1), grp, -jnp.inf)
            cand_ref[PEER_CAND_ROWS - 8:, :] = top_ref[0, 8:kk, :] + top_ref[1, 0:1, :]
            _top_values(cand_ref[...], kk, best_ref)
            best = best_ref[...]
            z = jnp.sum(jnp.exp(best - best[0:1, :]), axis=0, keepdims=True)
            thr = best[kk - 1:kk, :]
            m = jnp.zeros(s0.shape, F32)
            for b in range(4):
                m = m + jnp.where(s0 + top_ref[1, b:b + 1, :] >= thr, 1.0, 0.0)
            for a in range(3):
                tail = top_ref[0, a:a + 1, :] + top_ref[1, 4:kk // (a + 1), :]
                extra = jnp.sum(jnp.where(tail >= thr, 1.0, 0.0), axis=0, keepdims=True)
                m = m + jnp.where(s0 == top_ref[0, a:a + 1, :], extra, 0.0)
            m_ref[c, p] = m
            c0_ref[c, p] = jnp.exp(s0 - top_ref[0, 0:1, :]) / z
            e1_ref[c, p] = jnp.exp(s1 - top_ref[1, 0:1, :]).astype(BF16)
        return carry

    lax.fori_loop(0, m_ref.shape[0], chunk, 0)


def _peer_kernel(qp_ref, keys_ref, h2_ref, u_ref, vt_ref, x1_ref, gate_ref, fin_ref, o_ref,
                 m_ref, rank_ref, c0_ref, e1_ref, acc_ref, top_ref, cand_ref, best_ref, act_ref):
    eb = pl.program_id(1)
    n_eb = pl.num_programs(1)
    nk, tl = PEER_N_KEYS, PEER_LANES
    rows = u_ref.shape[0] // nk

    @pl.when(eb == 0)
    def _():
        acc_ref[...] = jnp.zeros_like(acc_ref)
        _peer_stats(qp_ref, keys_ref, m_ref, rank_ref, c0_ref, e1_ref, top_ref, cand_ref, best_ref)

    a_t = lax.dot_general(u_ref[...], h2_ref[...], NT_DIMS, preferred_element_type=F32)
    for r in range(rows):
        i = eb * rows + r
        sl = slice(r * nk, (r + 1) * nk)
        for c in range(m_ref.shape[0]):
            lanes = slice(c * tl, (c + 1) * tl)
            gmat = jnp.zeros((nk, tl), BF16)
            for p in range(PEER_HEADS):
                c0row = jnp.broadcast_to(c0_ref[c, p, pl.ds(i, 1), :].astype(BF16), (nk, tl))
                mrow = jnp.broadcast_to(m_ref[c, p, pl.ds(i, 1), :].astype(BF16), (nk, tl))
                val = c0row * e1_ref[c, p]
                gmat = gmat + jnp.where(rank_ref[c, p] < mrow, val, jnp.zeros_like(val))
            act_ref[sl, lanes] = _gelu(a_t[sl, lanes]).astype(BF16) * gmat
    acc_ref[...] += jnp.dot(vt_ref[0], act_ref[...], preferred_element_type=F32)

    @pl.when(eb == n_eb - 1)
    def _():
        x2 = x1_ref[...] + gate_ref[0] * acc_ref[...].T
        ms = jnp.mean(x2 * x2, axis=-1, keepdims=True)
        o_ref[...] = x2 * lax.rsqrt(ms + NORM_EPS) * fin_ref[...]


def _peer(qp, keys, h2, u, vt, x1, gate2, fin, seq, tm):
    t, d = x1.shape
    te = vt.shape[2]
    per_b = seq // tm
    nk, tl = PEER_N_KEYS, PEER_LANES
    stat = pltpu.VMEM((tm // tl, PEER_HEADS, nk, tl), F32)
    stat16 = pltpu.VMEM((tm // tl, PEER_HEADS, nk, tl), BF16)
    once = pl.Buffered(1)
    return pl.pallas_call(
        _peer_kernel,
        out_shape=jax.ShapeDtypeStruct((t, d), F32),
        grid=(t // tm, u.shape[0] // te),
        in_specs=[pl.BlockSpec((tm, qp.shape[1]), lambda i, e: (i, 0), pipeline_mode=once),
                  pl.BlockSpec(keys.shape, lambda i, e: (0, 0, 0), pipeline_mode=once),
                  pl.BlockSpec((tm, d), lambda i, e: (i, 0)),
                  pl.BlockSpec((te, d), lambda i, e: (e, 0)),
                  pl.BlockSpec((1, d, te), lambda i, e: (e, 0, 0)),
                  pl.BlockSpec((tm, d), lambda i, e: (i, 0), pipeline_mode=once),
                  pl.BlockSpec((1, 1, d), lambda i, e: (i // per_b, 0, 0)),
                  pl.BlockSpec((1, d), lambda i, e: (0, 0))],
        out_specs=pl.BlockSpec((tm, d), lambda i, e: (i, 0)),
        scratch_shapes=[stat, stat16, stat, stat16,
                        pltpu.VMEM((d, tm), F32),
                        pltpu.VMEM((2, PEER_TOPK, tl), F32),
                        pltpu.VMEM((PEER_CAND_ROWS, tl), F32),
                        pltpu.VMEM((PEER_TOPK, tl), F32),
                        pltpu.VMEM((te, tm), BF16)],
        compiler_params=_params(("arbitrary", "arbitrary")),
        name="peer",
    )(qp, keys, h2, u, vt, x1, gate2, fin.reshape(1, d))


def kernel(x, c, w_ada, b_ada, norm_mix, norm_ffn, w_in, lb_logits, hgrn_gain, w_up_a, w_up_b,
           w_out, peer_w_q, peer_keys, peer_u, peer_v, final_norm):
    bsz, seq, d = x.shape
    depth = w_ada.shape[0]
    assert depth == 1, "the PEER kernel applies the final norm, so it must be the last layer"
    aw = A_HEADS * HEAD_DIM
    bw = B_HEADS * HEAD_DIM
    iw = IDX_HEADS * IDX_DIM
    lower_bounds = jnp.cumsum(jax.nn.softmax(lb_logits.astype(F32), axis=0), axis=0)
    xt = x.reshape(bsz * seq, d)

    widths = (aw, aw, aw, iw, IDX_DIM, IDX_HEADS, bw, bw, bw, bw, d, d)
    names = ("qa", "ka", "va", "qi", "ki", "wi", "qb", "fb", "ib", "gb", "g_a", "g_b")
    off, start = {}, 0
    for nme, wd in zip(names, widths):
        off[nme] = (start, start + wd)
        start += wd
    order1 = ("g_a", "g_b", "qa", "ka", "va", "qi", "qb", "ib", "gb")
    order2 = ("fb", "ki", "wi")
    pad2 = LANES - IDX_DIM - IDX_HEADS

    def block_ids(order, block):
        ids, pos = {}, 0
        for nme in order:
            if pos % block == 0:
                ids[nme] = pos // block
            pos += off[nme][1] - off[nme][0]
        return ids

    for l in range(depth):
        mod = _ada(c, w_ada[l], b_ada[l], tn=TILES["ada_tn"])
        shift1, scale1, gate1, shift2, scale2, gate2 = [
            m.reshape(bsz, 1, d) for m in jnp.split(mod, 6, axis=-1)]

        wl = w_in[l].astype(BF16)
        w1 = jnp.concatenate([wl[:, off[n][0]:off[n][1]] for n in order1], axis=1)
        w2 = jnp.concatenate([wl[:, off[n][0]:off[n][1]] for n in order2]
                             + [jnp.zeros((d, pad2), BF16)], axis=1)
        p1, p2 = _inproj(xt, norm_mix[l], scale1, shift1, w1, w2, seq,
                         tm=TILES["inproj_tm"], tn=TILES["inproj_tn"])

        wide, head, lane2 = block_ids(order1, aw), block_ids(order1, HEAD_DIM), block_ids(order2, LANES)
        attn = _dsa(p1, p2, bsz, seq, {"qa": wide["qa"], "ka": wide["ka"], "va": wide["va"],
                                       "qi": wide["qi"], "kw": lane2["ki"]})
        rec = _hgrn(p1, p2, lower_bounds[l], hgrn_gain[l], bsz, seq,
                    {"qb": head["qb"], "ib": head["ib"], "gb": head["gb"], "fb": lane2["fb"]})

        x1, h2, qp = _mix(attn, rec, p1, xt, gate1,
                          w_up_a[l].astype(BF16), w_up_b[l].astype(BF16), w_out[l].astype(BF16),
                          norm_ffn[l], scale2, shift2, peer_w_q[l].astype(BF16),
                          seq, tm=TILES["mix_tm"], col=block_ids(order1, d))
        keys = peer_keys[l].reshape(2 * PEER_HEADS, PEER_N_KEYS, -1).astype(BF16)
        vt = peer_v[l].astype(BF16).reshape(-1, TILES["peer_te"], d).transpose(0, 2, 1)
        xt = _peer(qp, keys, h2, peer_u[l].astype(BF16), vt, x1, gate2, final_norm, seq,
                   tm=TILES["peer_tm"])
    return xt.reshape(bsz, seq, d)
```

```python
import jax
import jax.numpy as jnp
from jax import lax
from jax.experimental import pallas as pl
from jax.experimental.pallas import tpu as pltpu

F32 = jnp.float32
BF16 = jnp.bfloat16

NORM_EPS = 1e-6
CHUNK = 64
TOPK_MAX = 256
A_HEADS = 8
HEAD_DIM = 128
IDX_HEADS = 16
IDX_DIM = 64
B_HEADS = 8
PEER_HEADS = 8
PEER_N_KEYS = 128
PEER_TOPK = 16
SUB = 16

LANES = 128
VMEM_LIMIT = 56 * 1024 * 1024
INT_MIN = -2 ** 31

TILES = dict(
    ada_tn=1024,
    inproj_tm=1024, inproj_tn=1024,
    mix_tm=256,
    peer_tm=512,
    peer_te=1024,
)

NT_DIMS = (((1,), (1,)), ((), ()))
TN_DIMS = (((0,), (0,)), ((), ()))


def _sigmoid(x):
    return 1.0 / (1.0 + jnp.exp(-x))


def _params(sem):
    return pltpu.CompilerParams(dimension_semantics=sem, vmem_limit_bytes=VMEM_LIMIT)


def _ada_kernel(c_ref, w_ref, b_ref, o_ref):
    c = c_ref[...]
    cond = (c * _sigmoid(c)).astype(BF16)
    o_ref[...] = jnp.dot(cond, w_ref[...].astype(BF16), preferred_element_type=F32) + b_ref[...]


def _ada(c, w, b, tn):
    bsz, d = c.shape
    n = w.shape[1]
    return pl.pallas_call(
        _ada_kernel,
        out_shape=jax.ShapeDtypeStruct((bsz, n), F32),
        grid=(n // tn,),
        in_specs=[pl.BlockSpec((bsz, d), lambda j: (0, 0)),
                  pl.BlockSpec((d, tn), lambda j: (0, j)),
                  pl.BlockSpec((1, tn), lambda j: (0, j))],
        out_specs=pl.BlockSpec((bsz, tn), lambda j: (0, j)),
        compiler_params=_params(("arbitrary",)),
        name="ada",
    )(c, w, b.reshape(1, n))


def _norm_mod(x, gain, scale, shift):
    ms = jnp.mean(x * x, axis=-1, keepdims=True)
    y = x * lax.rsqrt(ms + NORM_EPS) * gain
    return y * (1.0 + scale) + shift


def _inproj_kernel(x_ref, g_ref, sc_ref, sh_ref, w1_ref, w2_ref, o1_ref, o2_ref, h_ref):
    j = pl.program_id(1)
    n1 = pl.num_programs(1) - 1

    @pl.when(j == 0)
    def _():
        h_ref[...] = _norm_mod(x_ref[...], g_ref[...], sc_ref[0], sh_ref[0]).astype(BF16)

    @pl.when(j < n1)
    def _():
        o1_ref[...] = jnp.dot(h_ref[...], w1_ref[...],
                              preferred_element_type=F32).astype(o1_ref.dtype)

    @pl.when(j == n1)
    def _():
        o2_ref[...] = jnp.dot(h_ref[...], w2_ref[...], preferred_element_type=F32)


def _inproj(x2d, gain, scale, shift, w1, w2, seq, tm, tn):
    t, d = x2d.shape
    n1 = w1.shape[1] // tn
    n2 = w2.shape[1]
    per_b = seq // tm
    return pl.pallas_call(
        _inproj_kernel,
        out_shape=(jax.ShapeDtypeStruct((t, w1.shape[1]), BF16),
                   jax.ShapeDtypeStruct((t, n2), F32)),
        grid=(t // tm, n1 + 1),
        in_specs=[pl.BlockSpec((tm, d), lambda i, j: (i, 0)),
                  pl.BlockSpec((1, d), lambda i, j: (0, 0)),
                  pl.BlockSpec((1, 1, d), lambda i, j: (i // per_b, 0, 0)),
                  pl.BlockSpec((1, 1, d), lambda i, j: (i // per_b, 0, 0)),
                  pl.BlockSpec((d, tn), lambda i, j: (0, jnp.minimum(j, n1 - 1))),
                  pl.BlockSpec((d, n2), lambda i, j: (0, 0))],
        out_specs=(pl.BlockSpec((tm, tn), lambda i, j: (i, jnp.minimum(j, n1 - 1))),
                   pl.BlockSpec((tm, n2), lambda i, j: (i, 0))),
        scratch_shapes=[pltpu.VMEM((tm, d), BF16)],
        compiler_params=_params(("arbitrary", "arbitrary")),
        name="inproj",
    )(x2d, gain.reshape(1, d), scale, shift, w1, w2)


DSA_QB = 256
DSA_KT = 256
DSA_AT = LANES
NEG_BIG = -1e30


def _dsa_kernel(qi_ref, kw_ref, qa_ref, ka_ref, va_ref, o_ref,
                key_ref, vt_ref, m_ref, l_ref, acc_ref):
    j = pl.program_id(1)
    qb, kt_rows = DSA_QB, DSA_KT
    seq = ka_ref.shape[0]
    n_kt = (j * qb + qb + kt_rows - 1) // kt_rows
    r0 = pl.multiple_of(j * qb, qb)

    @pl.when(j == 0)
    def _():
        for h in range(A_HEADS):
            for t in range(seq // kt_rows):
                blk = va_ref[t * kt_rows:(t + 1) * kt_rows, h * HEAD_DIM:(h + 1) * HEAD_DIM]
                vt_ref[h * HEAD_DIM:(h + 1) * HEAD_DIM, t * kt_rows:(t + 1) * kt_rows] = (
                    blk.astype(F32).T.astype(BF16))

    w_t = kw_ref[pl.ds(r0, qb), :].T
    q_chunk = (r0 + lax.broadcasted_iota(jnp.int32, (kt_rows, qb), 1)) // CHUNK
    key_iota = lax.broadcasted_iota(jnp.int32, (kt_rows, qb), 0)

    def score_tile(kt, carry):
        k0 = pl.multiple_of(kt * kt_rows, kt_rows)
        ki = kw_ref[pl.ds(k0, kt_rows), 0:IDX_DIM].astype(BF16)
        sc = jnp.zeros((kt_rows, qb), F32)
        for h in range(IDX_HEADS):
            dots = lax.dot_general(ki, qi_ref[:, h * IDX_DIM:(h + 1) * IDX_DIM], NT_DIMS,
                                   preferred_element_type=F32)
            sc = sc + w_t[IDX_DIM + h:IDX_DIM + h + 1, :] * jnp.maximum(dots, 0.0)
        sc = sc * ((IDX_DIM * IDX_HEADS) ** -0.5)
        bits = pltpu.bitcast(sc, jnp.int32)
        skey = bits ^ ((bits >> 31) & jnp.int32(0x7FFFFFFF))
        adm = (k0 + key_iota) // CHUNK <= q_chunk
        key_ref[pl.ds(k0, kt_rows), :] = jnp.where(adm, skey, jnp.int32(INT_MIN))
        return carry

    lax.fori_loop(0, n_kt, score_tile, 0)

    top_k = float(min(TOPK_MAX, seq // 4))

    def bit_step(it, v):
        cand = v + lax.shift_left(jnp.int32(1), 31 - it)

        def count_tile(kt, part):
            k0 = pl.multiple_of(kt * kt_rows, kt_rows)
            hit = jnp.where(key_ref[pl.ds(k0, kt_rows), :] >= cand, 1.0, 0.0)
            return part + jnp.sum(hit.reshape(kt_rows // 8, 8, qb), axis=0)

        part = lax.fori_loop(0, n_kt, count_tile, jnp.zeros((8, qb), F32))
        cnt = jnp.sum(part, axis=0, keepdims=True)
        return jnp.where(cnt >= top_k, cand, v)

    v = lax.fori_loop(0, 32, bit_step, jnp.full((1, qb), INT_MIN, jnp.int32))
    v = jnp.maximum(v, jnp.int32(INT_MIN + 1))

    att_scale = HEAD_DIM ** -0.5
    m_ref[...] = jnp.full(m_ref.shape, NEG_BIG, F32)
    l_ref[...] = jnp.zeros(l_ref.shape, F32)
    acc_ref[...] = jnp.zeros(acc_ref.shape, F32)

    at_rows = DSA_AT

    def att_tile(kt, carry):
        k0 = pl.multiple_of(kt * at_rows, at_rows)
        bias = jnp.where(key_ref[pl.ds(k0, at_rows), :] >= v, 0.0, NEG_BIG)
        for h in range(A_HEADS):
            cols = slice(h * HEAD_DIM, (h + 1) * HEAD_DIM)
            st = lax.dot_general(ka_ref[pl.ds(k0, at_rows), cols], qa_ref[:, cols], NT_DIMS,
                                 preferred_element_type=F32) * att_scale + bias
            m_old = m_ref[h:h + 1, :]
            m_new = jnp.maximum(m_old, jnp.max(st, axis=0, keepdims=True))
            alpha = jnp.exp(m_old - m_new)
            p = jnp.exp(st - m_new)
            m_ref[h:h + 1, :] = m_new
            l_ref[h:h + 1, :] = l_ref[h:h + 1, :] * alpha + jnp.sum(p, axis=0, keepdims=True)
            acc_ref[cols, :] = acc_ref[cols, :] * alpha + jnp.dot(
                vt_ref[cols, pl.ds(k0, at_rows)], p.astype(BF16), preferred_element_type=F32)
        return carry

    lax.fori_loop(0, n_kt * (kt_rows // at_rows), att_tile, 0)
    for h in range(A_HEADS):
        cols = slice(h * HEAD_DIM, (h + 1) * HEAD_DIM)
        o_ref[:, cols] = (acc_ref[cols, :] / l_ref[h:h + 1, :]).T.astype(o_ref.dtype)


def _dsa(p1, p2, bsz, seq, col):
    n_blk = seq // DSA_QB
    width = A_HEADS * HEAD_DIM
    return pl.pallas_call(
        _dsa_kernel,
        out_shape=jax.ShapeDtypeStruct((bsz * seq, width), BF16),
        grid=(bsz, n_blk),
        in_specs=[pl.BlockSpec((DSA_QB, width), lambda b, k: (b * n_blk + k, col["qi"])),
                  pl.BlockSpec((seq, LANES), lambda b, k: (b, col["kw"])),
                  pl.BlockSpec((DSA_QB, width), lambda b, k: (b * n_blk + k, col["qa"])),
                  pl.BlockSpec((seq, width), lambda b, k: (b, col["ka"])),
                  pl.BlockSpec((seq, width), lambda b, k: (b, col["va"]))],
        out_specs=pl.BlockSpec((DSA_QB, width), lambda b, k: (b * n_blk + k, 0)),
        scratch_shapes=[pltpu.VMEM((seq, DSA_QB), jnp.int32), pltpu.VMEM((width, seq), BF16),
                        pltpu.VMEM((A_HEADS, DSA_QB), F32), pltpu.VMEM((A_HEADS, DSA_QB), F32),
                        pltpu.VMEM((width, DSA_QB), F32)],
        compiler_params=_params(("arbitrary", "arbitrary")),
        name="dsa",
    )(p1, p2, p1, p1, p1)


HGRN_GROUP = 4


def _hgrn_kernel(q_ref, f_ref, i_ref, g_ref, lb_ref, gain_ref, o_ref,
                 qe_ref, oi_ref, upd_ref, dec_ref):
    c = CHUNK
    n_c = q_ref.shape[0] // c
    d = q_ref.shape[1]
    lb = lb_ref[0]
    gain = gain_ref[...]
    chunk_row = lax.broadcasted_iota(jnp.int32, (c, d), 0)
    sub_row = lax.broadcasted_iota(jnp.int32, (SUB, d), 0)

    def local_part(ci):
        r0 = pl.multiple_of(ci * c, c)
        f = lb + (1.0 - lb) * _sigmoid(f_ref[pl.ds(r0, c), :])
        logf = jnp.log(f)
        k = 1.0 - f
        qr = q_ref[pl.ds(r0, c), :].astype(F32)
        q = qr * _sigmoid(qr)
        v = i_ref[pl.ds(r0, c), :].astype(F32)
        v16 = v.astype(BF16)
        b = logf
        shift = 1
        while shift < c:
            b = b + jnp.where(chunk_row >= shift, pltpu.roll(b, shift, 0), 0.0)
            shift *= 2
        b_last = b[c - 1:c, :]
        qe_ref[pl.ds(r0, c), :] = (q * jnp.exp(b)).astype(BF16)

        for blk in range(c // SUB):
            lo = blk * SUB
            q_i, k_i, v_i, b_i = q[lo:lo + SUB], k[lo:lo + SUB], v[lo:lo + SUB], b[lo:lo + SUB]
            o_i = jnp.zeros((SUB, d), F32)
            if blk > 0:
                b_ref = b[lo - 1:lo, :]
                qt = (q_i * jnp.exp(b_i - b_ref)).astype(BF16)
                kt = (k[:lo] * jnp.exp(b_ref - b[:lo])).astype(BF16)
                att = lax.dot_general(qt, kt, NT_DIMS, preferred_element_type=F32)
                o_i = jnp.dot(att.astype(BF16), v16[:lo], preferred_element_type=F32)
            for s_ in range(SUB):
                w = jnp.exp(jnp.where(sub_row >= s_, b_i - b_i[s_:s_ + 1, :], -jnp.inf))
                a = jnp.sum(q_i * k_i[s_:s_ + 1, :] * w, axis=1, keepdims=True)
                o_i = o_i + a * v_i[s_:s_ + 1, :]
            oi_ref[pl.ds(r0 + lo, SUB), :] = o_i

        k_end = (k * jnp.exp(b_last - b)).astype(BF16)
        upd_ref[ci] = lax.dot_general(v16, k_end, TN_DIMS, preferred_element_type=F32)
        dec_ref[ci] = jnp.exp(b_last)

    def group(gi, carry):
        for u in range(HGRN_GROUP):
            local_part(gi * HGRN_GROUP + u)
        return carry

    lax.fori_loop(0, n_c // HGRN_GROUP, group, 0)

    st_t = jnp.zeros((d, d), F32)
    for ci in range(n_c):
        rows = slice(ci * c, (ci + 1) * c)
        o = oi_ref[rows, :] + lax.dot_general(qe_ref[rows, :], st_t.astype(BF16), NT_DIMS,
                                              preferred_element_type=F32)
        ms = jnp.mean(o * o, axis=-1, keepdims=True)
        rec = o * lax.rsqrt(ms + NORM_EPS) * gain
        gr = g_ref[rows, :].astype(F32)
        o_ref[rows, :] = (rec * (gr * _sigmoid(gr))).astype(o_ref.dtype)
        st_t = st_t * dec_ref[ci] + upd_ref[ci]


def _hgrn(p1, p2, lb, gain, bsz, seq, col):
    d = HEAD_DIM
    n_c = seq // CHUNK
    return pl.pallas_call(
        _hgrn_kernel,
        out_shape=jax.ShapeDtypeStruct((bsz * seq, B_HEADS * d), BF16),
        grid=(bsz, B_HEADS),
        in_specs=[pl.BlockSpec((seq, d), lambda b, h: (b, col["qb"] + h)),
                  pl.BlockSpec((seq, d), lambda b, h: (b, col["fb"] + h)),
                  pl.BlockSpec((seq, d), lambda b, h: (b, col["ib"] + h)),
                  pl.BlockSpec((seq, d), lambda b, h: (b, col["gb"] + h)),
                  pl.BlockSpec((1, 1, d), lambda b, h: (h, 0, 0)),
                  pl.BlockSpec((1, d), lambda b, h: (0, 0))],
        out_specs=pl.BlockSpec((seq, d), lambda b, h: (b, h)),
        scratch_shapes=[pltpu.VMEM((seq, d), BF16), pltpu.VMEM((seq, d), F32),
                        pltpu.VMEM((n_c, d, d), F32), pltpu.VMEM((n_c, 1, d), F32)],
        compiler_params=_params(("arbitrary", "arbitrary")),
        name="hgrn",
    )(p1, p2, p1, p1, lb.reshape(B_HEADS, 1, d), gain.reshape(1, d))


def _mix_kernel(attn_ref, rec_ref, ga_ref, gb_ref, x_ref, gate_ref, wa_ref, wb_ref, wo_ref,
                g2_ref, sc_ref, sh_ref, wq_ref, x1_ref, h2_ref, qp_ref):
    ya = jnp.dot(attn_ref[...], wa_ref[...], preferred_element_type=F32)
    yb = jnp.dot(rec_ref[...], wb_ref[...], preferred_element_type=F32)
    mixed = _sigmoid(ga_ref[...].astype(F32)) * ya + _sigmoid(gb_ref[...].astype(F32)) * yb
    x1 = x_ref[...] + gate_ref[0] * jnp.dot(mixed.astype(BF16), wo_ref[...],
                                            preferred_element_type=F32)
    x1_ref[...] = x1
    h2 = _norm_mod(x1, g2_ref[...], sc_ref[0], sh_ref[0]).astype(BF16)
    h2_ref[...] = h2
    qp_ref[...] = jnp.dot(h2, wq_ref[...], preferred_element_type=F32).astype(qp_ref.dtype)


def _mix(attn, rec, p1, x2d, gate1, wa, wb, wo, g2, scale2, shift2, wq, seq, tm, col):
    t, d = x2d.shape
    wa_w = attn.shape[1]
    per_b = seq // tm
    full = lambda shape: pl.BlockSpec(shape, lambda i: (0,) * len(shape),
                                      pipeline_mode=pl.Buffered(1))
    mod = pl.BlockSpec((1, 1, d), lambda i: (i // per_b, 0, 0))
    return pl.pallas_call(
        _mix_kernel,
        out_shape=(jax.ShapeDtypeStruct((t, d), F32), jax.ShapeDtypeStruct((t, d), BF16),
                   jax.ShapeDtypeStruct((t, wq.shape[1]), BF16)),
        grid=(t // tm,),
        in_specs=[pl.BlockSpec((tm, wa_w), lambda i: (i, 0)),
                  pl.BlockSpec((tm, wa_w), lambda i: (i, 0)),
                  pl.BlockSpec((tm, d), lambda i: (i, col["g_a"])),
                  pl.BlockSpec((tm, d), lambda i: (i, col["g_b"])),
                  pl.BlockSpec((tm, d), lambda i: (i, 0)),
                  mod, full(wa.shape), full(wb.shape), full(wo.shape),
                  full((1, d)), mod, mod, full(wq.shape)],
        out_specs=(pl.BlockSpec((tm, d), lambda i: (i, 0)),
                   pl.BlockSpec((tm, d), lambda i: (i, 0)),
                   pl.BlockSpec((tm, wq.shape[1]), lambda i: (i, 0))),
        compiler_params=_params(("arbitrary",)),
        name="mix",
    )(attn, rec, p1, p1, x2d, gate1, wa, wb, wo, g2.reshape(1, d), scale2, shift2, wq)


def _top_values(work, n, dst_ref):
    rank = jnp.full(work.shape, float(n), F32)
    for r in range(n):
        mx = jnp.max(work, axis=0, keepdims=True)
        dst_ref[r:r + 1, :] = mx
        hit = work == mx
        rank = jnp.where(hit, float(r), rank)
        work = jnp.where(hit, -jnp.inf, work)
    return rank


def _gelu(x):
    return 0.5 * x * (1.0 + lax.erf(x * (2.0 ** -0.5)))


PEER_CAND_ROWS = 5 * PEER_TOPK
PEER_LANES = LANES


def _peer_stats(qp_ref, keys_ref, m_ref, rank_ref, c0_ref, e1_ref, top_ref, cand_ref, best_ref):
    nk, kk, tl = PEER_N_KEYS, PEER_TOPK, PEER_LANES
    row8 = lax.broadcasted_iota(jnp.int32, (8, tl), 0)

    def chunk(c, carry):
        t0 = pl.multiple_of(c * tl, tl)
        for p in range(PEER_HEADS):
            scores = []
            for half in range(2):
                j = 2 * p + half
                qs = qp_ref[pl.ds(t0, tl), j * nk:(j + 1) * nk]
                scores.append(lax.dot_general(keys_ref[j], qs, NT_DIMS,
                                              preferred_element_type=F32))
            s0, s1 = scores
            _top_values(s0, kk, top_ref.at[0])
            rank_ref[c, p] = _top_values(s1, kk, top_ref.at[1])
            cand_ref[0:kk, :] = top_ref[0, 0:1, :] + top_ref[1]
            cand_ref[kk:kk + 8, :] = top_ref[0, 1:2, :] + top_ref[1, 0:8, :]
            for a in range(2, 8):
                grp = top_ref[0, a:a + 1, :] + top_ref[1, 0:8, :]
                cand_ref[8 + 8 * a:16 + 8 * a, :] = jnp.where(row8 < kk // (a + 1), grp, -jnp.inf)
            cand_ref[PEER_CAND_ROWS - 8:, :] = top_ref[0, 8:kk, :] + top_ref[1, 0:1, :]
            _top_values(cand_ref[...], kk, best_ref)
            best = best_ref[...]
            z = jnp.sum(jnp.exp(best - best[0:1, :]), axis=0, keepdims=True)
            thr = best[kk - 1:kk, :]
            m = jnp.zeros(s0.shape, F32)
            for b in range(4):
                m = m + jnp.where(s0 + top_ref[1, b:b + 1, :] >= thr, 1.0, 0.0)
            for a in range(3):
                tail = top_ref[0, a:a + 1, :] + top_ref[1, 4:kk // (a + 1), :]
                extra = jnp.sum(jnp.where(tail >= thr, 1.0, 0.0), axis=0, keepdims=True)
                m = m + jnp.where(s0 == top_ref[0, a:a + 1, :], extra, 0.0)
            m_ref[c, p] = m
            c0_ref[c, p] = jnp.exp(s0 - top_ref[0, 0:1, :]) / z
            e1_ref[c, p] = jnp.exp(s1 - top_ref[1, 0:1, :])
        return carry

    lax.fori_loop(0, m_ref.shape[0], chunk, 0)


def _peer_kernel(qp_ref, keys_ref, h2_ref, u_ref, vt_ref, x1_ref, gate_ref, fin_ref, o_ref,
                 m_ref, rank_ref, c0_ref, e1_ref, acc_ref, top_ref, cand_ref, best_ref, act_ref):
    eb = pl.program_id(1)
    n_eb = pl.num_programs(1)
    nk, tl = PEER_N_KEYS, PEER_LANES
    rows = u_ref.shape[0] // nk

    @pl.when(eb == 0)
    def _():
        acc_ref[...] = jnp.zeros_like(acc_ref)
        _peer_stats(qp_ref, keys_ref, m_ref, rank_ref, c0_ref, e1_ref, top_ref, cand_ref, best_ref)

    a_t = lax.dot_general(u_ref[...], h2_ref[...], NT_DIMS, preferred_element_type=F32)
    for r in range(rows):
        i = eb * rows + r
        sl = slice(r * nk, (r + 1) * nk)
        for c in range(m_ref.shape[0]):
            lanes = slice(c * tl, (c + 1) * tl)
            gmat = jnp.zeros((nk, tl), F32)
            for p in range(PEER_HEADS):
                c0row = c0_ref[c, p, pl.ds(i, 1), :]
                mrow = m_ref[c, p, pl.ds(i, 1), :]
                val = c0row * e1_ref[c, p]
                gmat = gmat + jnp.where(rank_ref[c, p] < mrow, val, 0.0)
            act_ref[sl, lanes] = (_gelu(a_t[sl, lanes]) * gmat).astype(BF16)
    acc_ref[...] += jnp.dot(vt_ref[0], act_ref[...], preferred_element_type=F32)

    @pl.when(eb == n_eb - 1)
    def _():
        x2 = x1_ref[...] + gate_ref[0] * acc_ref[...].T
        ms = jnp.mean(x2 * x2, axis=-1, keepdims=True)
        o_ref[...] = x2 * lax.rsqrt(ms + NORM_EPS) * fin_ref[...]


def _peer(qp, keys, h2, u, vt, x1, gate2, fin, seq, tm):
    t, d = x1.shape
    te = vt.shape[2]
    per_b = seq // tm
    nk, tl = PEER_N_KEYS, PEER_LANES
    stat = pltpu.VMEM((tm // tl, PEER_HEADS, nk, tl), F32)
    stat16 = pltpu.VMEM((tm // tl, PEER_HEADS, nk, tl), BF16)
    once = pl.Buffered(1)
    return pl.pallas_call(
        _peer_kernel,
        out_shape=jax.ShapeDtypeStruct((t, d), F32),
        grid=(t // tm, u.shape[0] // te),
        in_specs=[pl.BlockSpec((tm, qp.shape[1]), lambda i, e: (i, 0), pipeline_mode=once),
                  pl.BlockSpec(keys.shape, lambda i, e: (0, 0, 0), pipeline_mode=once),
                  pl.BlockSpec((tm, d), lambda i, e: (i, 0)),
                  pl.BlockSpec((te, d), lambda i, e: (e, 0)),
                  pl.BlockSpec((1, d, te), lambda i, e: (e, 0, 0)),
                  pl.BlockSpec((tm, d), lambda i, e: (i, 0), pipeline_mode=once),
                  pl.BlockSpec((1, 1, d), lambda i, e: (i // per_b, 0, 0)),
                  pl.BlockSpec((1, d), lambda i, e: (0, 0))],
        out_specs=pl.BlockSpec((tm, d), lambda i, e: (i, 0)),
        scratch_shapes=[stat, stat, stat, stat,
                        pltpu.VMEM((d, tm), F32),
                        pltpu.VMEM((2, PEER_TOPK, tl), F32),
                        pltpu.VMEM((PEER_CAND_ROWS, tl), F32),
                        pltpu.VMEM((PEER_TOPK, tl), F32),
                        pltpu.VMEM((te, tm), BF16)],
        compiler_params=_params(("arbitrary", "arbitrary")),
        name="peer",
    )(qp, keys, h2, u, vt, x1, gate2, fin.reshape(1, d))


def kernel(x, c, w_ada, b_ada, norm_mix, norm_ffn, w_in, lb_logits, hgrn_gain, w_up_a, w_up_b,
           w_out, peer_w_q, peer_keys, peer_u, peer_v, final_norm):
    bsz, seq, d = x.shape
    depth = w_ada.shape[0]
    assert depth == 1, "the PEER kernel applies the final norm, so it must be the last layer"
    aw = A_HEADS * HEAD_DIM
    bw = B_HEADS * HEAD_DIM
    iw = IDX_HEADS * IDX_DIM
    lower_bounds = jnp.cumsum(jax.nn.softmax(lb_logits.astype(F32), axis=0), axis=0)
    xt = x.reshape(bsz * seq, d)

    widths = (aw, aw, aw, iw, IDX_DIM, IDX_HEADS, bw, bw, bw, bw, d, d)
    names = ("qa", "ka", "va", "qi", "ki", "wi", "qb", "fb", "ib", "gb", "g_a", "g_b")
    off, start = {}, 0
    for nme, wd in zip(names, widths):
        off[nme] = (start, start + wd)
        start += wd
    order1 = ("g_a", "g_b", "qa", "ka", "va", "qi", "qb", "ib", "gb")
    order2 = ("fb", "ki", "wi")
    pad2 = LANES - IDX_DIM - IDX_HEADS

    def block_ids(order, block):
        ids, pos = {}, 0
        for nme in order:
            if pos % block == 0:
                ids[nme] = pos // block
            pos += off[nme][1] - off[nme][0]
        return ids

    for l in range(depth):
        mod = _ada(c, w_ada[l], b_ada[l], tn=TILES["ada_tn"])
        shift1, scale1, gate1, shift2, scale2, gate2 = [
            m.reshape(bsz, 1, d) for m in jnp.split(mod, 6, axis=-1)]

        wl = w_in[l].astype(BF16)
        w1 = jnp.concatenate([wl[:, off[n][0]:off[n][1]] for n in order1], axis=1)
        w2 = jnp.concatenate([wl[:, off[n][0]:off[n][1]] for n in order2]
                             + [jnp.zeros((d, pad2), BF16)], axis=1)
        p1, p2 = _inproj(xt, norm_mix[l], scale1, shift1, w1, w2, seq,
                         tm=TILES["inproj_tm"], tn=TILES["inproj_tn"])

        wide, head, lane2 = block_ids(order1, aw), block_ids(order1, HEAD_DIM), block_ids(order2, LANES)
        attn = _dsa(p1, p2, bsz, seq, {"qa": wide["qa"], "ka": wide["ka"], "va": wide["va"],
                                       "qi": wide["qi"], "kw": lane2["ki"]})
        rec = _hgrn(p1, p2, lower_bounds[l], hgrn_gain[l], bsz, seq,
                    {"qb": head["qb"], "ib": head["ib"], "gb": head["gb"], "fb": lane2["fb"]})

        x1, h2, qp = _mix(attn, rec, p1, xt, gate1,
                          w_up_a[l].astype(BF16), w_up_b[l].astype(BF16), w_out[l].astype(BF16),
                          norm_ffn[l], scale2, shift2, peer_w_q[l].astype(BF16),
                          seq, tm=TILES["mix_tm"], col=block_ids(order1, d))
        keys = peer_keys[l].reshape(2 * PEER_HEADS, PEER_N_KEYS, -1).astype(BF16)
        vt = peer_v[l].astype(BF16).reshape(-1, TILES["peer_te"], d).transpose(0, 2, 1)
        xt = _peer(qp, keys, h2, peer_u[l].astype(BF16), vt, x1, gate2, final_norm, seq,
                   tm=TILES["peer_tm"])
    return xt.reshape(bsz, seq, d)
```

```python
import jax
import jax.numpy as jnp
from jax import lax
from jax.experimental import pallas as pl
from jax.experimental.pallas import tpu as pltpu

F32 = jnp.float32
BF16 = jnp.bfloat16

NORM_EPS = 1e-6
CHUNK = 64
TOPK_MAX = 256
A_HEADS = 8
HEAD_DIM = 128
IDX_HEADS = 16
IDX_DIM = 64
B_HEADS = 8
PEER_HEADS = 8
PEER_N_KEYS = 128
PEER_TOPK = 16
SUB = 16

LANES = 128
VMEM_LIMIT = 56 * 1024 * 1024
INT_MIN = -2 ** 31

TILES = dict(
    ada_tn=1024,
    inproj_tm=1024, inproj_tn=1024,
    mix_tm=256,
    peer_tm=512,
    peer_te=1024,
)

NT_DIMS = (((1,), (1,)), ((), ()))
TN_DIMS = (((0,), (0,)), ((), ()))


def _sigmoid(x):
    return 1.0 / (1.0 + jnp.exp(-x))


def _params(sem):
    return pltpu.CompilerParams(dimension_semantics=sem, vmem_limit_bytes=VMEM_LIMIT)


def _ada_kernel(c_ref, w_ref, b_ref, o_ref):
    c = c_ref[...]
    cond = (c * _sigmoid(c)).astype(BF16)
    o_ref[...] = jnp.dot(cond, w_ref[...].astype(BF16), preferred_element_type=F32) + b_ref[...]


def _ada(c, w, b, tn):
    bsz, d = c.shape
    n = w.shape[1]
    return pl.pallas_call(
        _ada_kernel,
        out_shape=jax.ShapeDtypeStruct((bsz, n), F32),
        grid=(n // tn,),
        in_specs=[pl.BlockSpec((bsz, d), lambda j: (0, 0)),
                  pl.BlockSpec((d, tn), lambda j: (0, j)),
                  pl.BlockSpec((1, tn), lambda j: (0, j))],
        out_specs=pl.BlockSpec((bsz, tn), lambda j: (0, j)),
        compiler_params=_params(("arbitrary",)),
        name="ada",
    )(c, w, b.reshape(1, n))


def _norm_mod(x, gain, scale, shift):
    ms = jnp.mean(x * x, axis=-1, keepdims=True)
    y = x * lax.rsqrt(ms + NORM_EPS) * gain
    return y * (1.0 + scale) + shift


def _inproj_kernel(x_ref, g_ref, sc_ref, sh_ref, w1_ref, w2_ref, o1_ref, o2_ref, h_ref):
    j = pl.program_id(1)
    n1 = pl.num_programs(1) - 1

    @pl.when(j == 0)
    def _():
        h_ref[...] = _norm_mod(x_ref[...], g_ref[...], sc_ref[0], sh_ref[0]).astype(BF16)

    @pl.when(j < n1)
    def _():
        o1_ref[...] = jnp.dot(h_ref[...], w1_ref[...],
                              preferred_element_type=F32).astype(o1_ref.dtype)

    @pl.when(j == n1)
    def _():
        o2_ref[...] = jnp.dot(h_ref[...], w2_ref[...], preferred_element_type=F32)


def _inproj(x2d, gain, scale, shift, w1, w2, seq, tm, tn):
    t, d = x2d.shape
    n1 = w1.shape[1] // tn
    n2 = w2.shape[1]
    per_b = seq // tm
    return pl.pallas_call(
        _inproj_kernel,
        out_shape=(jax.ShapeDtypeStruct((t, w1.shape[1]), BF16),
                   jax.ShapeDtypeStruct((t, n2), F32)),
        grid=(t // tm, n1 + 1),
        in_specs=[pl.BlockSpec((tm, d), lambda i, j: (i, 0)),
                  pl.BlockSpec((1, d), lambda i, j: (0, 0)),
                  pl.BlockSpec((1, 1, d), lambda i, j: (i // per_b, 0, 0)),
                  pl.BlockSpec((1, 1, d), lambda i, j: (i // per_b, 0, 0)),
                  pl.BlockSpec((d, tn), lambda i, j: (0, jnp.minimum(j, n1 - 1))),
                  pl.BlockSpec((d, n2), lambda i, j: (0, 0))],
        out_specs=(pl.BlockSpec((tm, tn), lambda i, j: (i, jnp.minimum(j, n1 - 1))),
                   pl.BlockSpec((tm, n2), lambda i, j: (i, 0))),
        scratch_shapes=[pltpu.VMEM((tm, d), BF16)],
        compiler_params=_params(("arbitrary", "arbitrary")),
        name="inproj",
    )(x2d, gain.reshape(1, d), scale, shift, w1, w2)


DSA_QB = 256
DSA_KT = 256
DSA_AT = LANES
NEG_BIG = -1e30


def _dsa_kernel(qi_ref, kw_ref, qa_ref, ka_ref, va_ref, o_ref,
                key_ref, vt_ref, m_ref, l_ref, acc_ref):
    j = pl.program_id(1)
    qb, kt_rows = DSA_QB, DSA_KT
    seq = ka_ref.shape[0]
    n_kt = (j * qb + qb + kt_rows - 1) // kt_rows
    r0 = pl.multiple_of(j * qb, qb)

    @pl.when(j == 0)
    def _():
        for h in range(A_HEADS):
            for t in range(seq // kt_rows):
                blk = va_ref[t * kt_rows:(t + 1) * kt_rows, h * HEAD_DIM:(h + 1) * HEAD_DIM]
                vt_ref[h * HEAD_DIM:(h + 1) * HEAD_DIM, t * kt_rows:(t + 1) * kt_rows] = (
                    blk.astype(F32).T.astype(BF16))

    w_t = kw_ref[pl.ds(r0, qb), :].T
    q_chunk = (r0 + lax.broadcasted_iota(jnp.int32, (kt_rows, qb), 1)) // CHUNK
    key_iota = lax.broadcasted_iota(jnp.int32, (kt_rows, qb), 0)

    def score_tile(kt, carry):
        k0 = pl.multiple_of(kt * kt_rows, kt_rows)
        ki = kw_ref[pl.ds(k0, kt_rows), 0:IDX_DIM].astype(BF16)
        sc = jnp.zeros((kt_rows, qb), F32)
        for h in range(IDX_HEADS):
            dots = lax.dot_general(ki, qi_ref[:, h * IDX_DIM:(h + 1) * IDX_DIM], NT_DIMS,
                                   preferred_element_type=F32)
            sc = sc + w_t[IDX_DIM + h:IDX_DIM + h + 1, :] * jnp.maximum(dots, 0.0)
        sc = sc * ((IDX_DIM * IDX_HEADS) ** -0.5)
        bits = pltpu.bitcast(sc, jnp.int32)
        skey = bits ^ ((bits >> 31) & jnp.int32(0x7FFFFFFF))
        adm = (k0 + key_iota) // CHUNK <= q_chunk
        key_ref[pl.ds(k0, kt_rows), :] = jnp.where(adm, skey, jnp.int32(INT_MIN))
        return carry

    lax.fori_loop(0, n_kt, score_tile, 0)

    top_k = float(min(TOPK_MAX, seq // 4))

    def bit_step(it, v):
        cand = v + lax.shift_left(jnp.int32(1), 31 - it)

        def count_tile(kt, part):
            k0 = pl.multiple_of(kt * kt_rows, kt_rows)
            hit = jnp.where(key_ref[pl.ds(k0, kt_rows), :] >= cand, 1.0, 0.0)
            return part + jnp.sum(hit.reshape(kt_rows // 8, 8, qb), axis=0)

        part = lax.fori_loop(0, n_kt, count_tile, jnp.zeros((8, qb), F32))
        cnt = jnp.sum(part, axis=0, keepdims=True)
        return jnp.where(cnt >= top_k, cand, v)

    v = lax.fori_loop(0, 32, bit_step, jnp.full((1, qb), INT_MIN, jnp.int32))
    v = jnp.maximum(v, jnp.int32(INT_MIN + 1))

    att_scale = HEAD_DIM ** -0.5
    m_ref[...] = jnp.full(m_ref.shape, NEG_BIG, F32)
    l_ref[...] = jnp.zeros(l_ref.shape, F32)
    acc_ref[...] = jnp.zeros(acc_ref.shape, F32)

    at_rows = DSA_AT

    def att_tile(kt, carry):
        k0 = pl.multiple_of(kt * at_rows, at_rows)
        bias = jnp.where(key_ref[pl.ds(k0, at_rows), :] >= v, 0.0, NEG_BIG)
        for h in range(A_HEADS):
            cols = slice(h * HEAD_DIM, (h + 1) * HEAD_DIM)
            st = lax.dot_general(ka_ref[pl.ds(k0, at_rows), cols], qa_ref[:, cols], NT_DIMS,
                                 preferred_element_type=F32) * att_scale + bias
            m_old = m_ref[h:h + 1, :]
            m_new = jnp.maximum(m_old, jnp.max(st, axis=0, keepdims=True))
            alpha = jnp.exp(m_old - m_new)
            p = jnp.exp(st - m_new)
            m_ref[h:h + 1, :] = m_new
            l_ref[h:h + 1, :] = l_ref[h:h + 1, :] * alpha + jnp.sum(p, axis=0, keepdims=True)
            acc_ref[cols, :] = acc_ref[cols, :] * alpha + jnp.dot(
                vt_ref[cols, pl.ds(k0, at_rows)], p.astype(BF16), preferred_element_type=F32)
        return carry

    lax.fori_loop(0, n_kt * (kt_rows // at_rows), att_tile, 0)
    for h in range(A_HEADS):
        cols = slice(h * HEAD_DIM, (h + 1) * HEAD_DIM)
        o_ref[:, cols] = (acc_ref[cols, :] / l_ref[h:h + 1, :]).T.astype(o_ref.dtype)


def _dsa(p1, p2, bsz, seq, col):
    n_blk = seq // DSA_QB
    width = A_HEADS * HEAD_DIM
    return pl.pallas_call(
        _dsa_kernel,
        out_shape=jax.ShapeDtypeStruct((bsz * seq, width), BF16),
        grid=(bsz, n_blk),
        in_specs=[pl.BlockSpec((DSA_QB, width), lambda b, k: (b * n_blk + k, col["qi"])),
                  pl.BlockSpec((seq, LANES), lambda b, k: (b, col["kw"])),
                  pl.BlockSpec((DSA_QB, width), lambda b, k: (b * n_blk + k, col["qa"])),
                  pl.BlockSpec((seq, width), lambda b, k: (b, col["ka"])),
                  pl.BlockSpec((seq, width), lambda b, k: (b, col["va"]))],
        out_specs=pl.BlockSpec((DSA_QB, width), lambda b, k: (b * n_blk + k, 0)),
        scratch_shapes=[pltpu.VMEM((seq, DSA_QB), jnp.int32), pltpu.VMEM((width, seq), BF16),
                        pltpu.VMEM((A_HEADS, DSA_QB), F32), pltpu.VMEM((A_HEADS, DSA_QB), F32),
                        pltpu.VMEM((width, DSA_QB), F32)],
        compiler_params=_params(("arbitrary", "arbitrary")),
        name="dsa",
    )(p1, p2, p1, p1, p1)


HGRN_GROUP = 4


def _hgrn_kernel(q_ref, f_ref, i_ref, g_ref, lb_ref, gain_ref, o_ref,
                 qe_ref, oi_ref, upd_ref, dec_ref):
    c = CHUNK
    n_c = q_ref.shape[0] // c
    d = q_ref.shape[1]
    lb = lb_ref[0]
    gain = gain_ref[...]
    chunk_row = lax.broadcasted_iota(jnp.int32, (c, d), 0)
    sub_row = lax.broadcasted_iota(jnp.int32, (SUB, d), 0)

    def local_part(ci):
        r0 = pl.multiple_of(ci * c, c)
        f = lb + (1.0 - lb) * _sigmoid(f_ref[pl.ds(r0, c), :])
        logf = jnp.log(f)
        k = 1.0 - f
        qr = q_ref[pl.ds(r0, c), :].astype(F32)
        q = qr * _sigmoid(qr)
        v = i_ref[pl.ds(r0, c), :].astype(F32)
        v16 = v.astype(BF16)
        b = logf
        shift = 1
        while shift < c:
            b = b + jnp.where(chunk_row >= shift, pltpu.roll(b, shift, 0), 0.0)
            shift *= 2
        b_last = b[c - 1:c, :]
        qe_ref[pl.ds(r0, c), :] = (q * jnp.exp(b)).astype(BF16)

        for blk in range(c // SUB):
            lo = blk * SUB
            q_i, k_i, v_i, b_i = q[lo:lo + SUB], k[lo:lo + SUB], v[lo:lo + SUB], b[lo:lo + SUB]
            o_i = jnp.zeros((SUB, d), F32)
            if blk > 0:
                b_ref = b[lo - 1:lo, :]
                qt = (q_i * jnp.exp(b_i - b_ref)).astype(BF16)
                kt = (k[:lo] * jnp.exp(b_ref - b[:lo])).astype(BF16)
                att = lax.dot_general(qt, kt, NT_DIMS, preferred_element_type=F32)
                o_i = jnp.dot(att.astype(BF16), v16[:lo], preferred_element_type=F32)
            for s_ in range(SUB):
                w = jnp.exp(jnp.where(sub_row >= s_, b_i - b_i[s_:s_ + 1, :], -jnp.inf))
                a = jnp.sum(q_i * k_i[s_:s_ + 1, :] * w, axis=1, keepdims=True)
                o_i = o_i + a * v_i[s_:s_ + 1, :]
            oi_ref[pl.ds(r0 + lo, SUB), :] = o_i

        k_end = (k * jnp.exp(b_last - b)).astype(BF16)
        upd_ref[ci] = lax.dot_general(v16, k_end, TN_DIMS, preferred_element_type=F32)
        dec_ref[ci] = jnp.exp(b_last)

    def group(gi, carry):
        for u in range(HGRN_GROUP):
            local_part(gi * HGRN_GROUP + u)
        return carry

    lax.fori_loop(0, n_c // HGRN_GROUP, group, 0)

    st_t = jnp.zeros((d, d), F32)
    for ci in range(n_c):
        rows = slice(ci * c, (ci + 1) * c)
        o = oi_ref[rows, :] + lax.dot_general(qe_ref[rows, :], st_t.astype(BF16), NT_DIMS,
                                              preferred_element_type=F32)
        ms = jnp.mean(o * o, axis=-1, keepdims=True)
        rec = o * lax.rsqrt(ms + NORM_EPS) * gain
        gr = g_ref[rows, :].astype(F32)
        o_ref[rows, :] = (rec * (gr * _sigmoid(gr))).astype(o_ref.dtype)
        st_t = st_t * dec_ref[ci] + upd_ref[ci]


def _hgrn(p1, p2, lb, gain, bsz, seq, col):
    d = HEAD_DIM
    n_c = seq // CHUNK
    return pl.pallas_call(
        _hgrn_kernel,
        out_shape=jax.ShapeDtypeStruct((bsz * seq, B_HEADS * d), BF16),
        grid=(bsz, B_HEADS),
        in_specs=[pl.BlockSpec((seq, d), lambda b, h: (b, col["qb"] + h)),
                  pl.BlockSpec((seq, d), lambda b, h: (b, col["fb"] + h)),
                  pl.BlockSpec((seq, d), lambda b, h: (b, col["ib"] + h)),
                  pl.BlockSpec((seq, d), lambda b, h: (b, col["gb"] + h)),
                  pl.BlockSpec((1, 1, d), lambda b, h: (h, 0, 0)),
                  pl.BlockSpec((1, d), lambda b, h: (0, 0))],
        out_specs=pl.BlockSpec((seq, d), lambda b, h: (b, h)),
        scratch_shapes=[pltpu.VMEM((seq, d), BF16), pltpu.VMEM((seq, d), F32),
                        pltpu.VMEM((n_c, d, d), F32), pltpu.VMEM((n_c, 1, d), F32)],
        compiler_params=_params(("arbitrary", "arbitrary")),
        name="hgrn",
    )(p1, p2, p1, p1, lb.reshape(B_HEADS, 1, d), gain.reshape(1, d))


def _mix_kernel(attn_ref, rec_ref, ga_ref, gb_ref, x_ref, gate_ref, wa_ref, wb_ref, wo_ref,
                g2_ref, sc_ref, sh_ref, wq_ref, x1_ref, h2_ref, qp_ref):
    ya = jnp.dot(attn_ref[...], wa_ref[...], preferred_element_type=F32)
    yb = jnp.dot(rec_ref[...], wb_ref[...], preferred_element_type=F32)
    mixed = _sigmoid(ga_ref[...].astype(F32)) * ya + _sigmoid(gb_ref[...].astype(F32)) * yb
    x1 = x_ref[...] + gate_ref[0] * jnp.dot(mixed.astype(BF16), wo_ref[...],
                                            preferred_element_type=F32)
    x1_ref[...] = x1
    h2 = _norm_mod(x1, g2_ref[...], sc_ref[0], sh_ref[0]).astype(BF16)
    h2_ref[...] = h2
    qp_ref[...] = jnp.dot(h2, wq_ref[...], preferred_element_type=F32).astype(qp_ref.dtype)


def _mix(attn, rec, p1, x2d, gate1, wa, wb, wo, g2, scale2, shift2, wq, seq, tm, col):
    t, d = x2d.shape
    wa_w = attn.shape[1]
    per_b = seq // tm
    full = lambda shape: pl.BlockSpec(shape, lambda i: (0,) * len(shape),
                                      pipeline_mode=pl.Buffered(1))
    mod = pl.BlockSpec((1, 1, d), lambda i: (i // per_b, 0, 0))
    return pl.pallas_call(
        _mix_kernel,
        out_shape=(jax.ShapeDtypeStruct((t, d), F32), jax.ShapeDtypeStruct((t, d), BF16),
                   jax.ShapeDtypeStruct((t, wq.shape[1]), BF16)),
        grid=(t // tm,),
        in_specs=[pl.BlockSpec((tm, wa_w), lambda i: (i, 0)),
                  pl.BlockSpec((tm, wa_w), lambda i: (i, 0)),
                  pl.BlockSpec((tm, d), lambda i: (i, col["g_a"])),
                  pl.BlockSpec((tm, d), lambda i: (i, col["g_b"])),
                  pl.BlockSpec((tm, d), lambda i: (i, 0)),
                  mod, full(wa.shape), full(wb.shape), full(wo.shape),
                  full((1, d)), mod, mod, full(wq.shape)],
        out_specs=(pl.BlockSpec((tm, d), lambda i: (i, 0)),
                   pl.BlockSpec((tm, d), lambda i: (i, 0)),
                   pl.BlockSpec((tm, wq.shape[1]), lambda i: (i, 0))),
        compiler_params=_params(("arbitrary",)),
        name="mix",
    )(attn, rec, p1, p1, x2d, gate1, wa, wb, wo, g2.reshape(1, d), scale2, shift2, wq)


def _top_values(work, n, dst_ref):
    rank = jnp.full(work.shape, float(n), F32)
    for r in range(n):
        mx = jnp.max(work, axis=0, keepdims=True)
        dst_ref[r:r + 1, :] = mx
        hit = work == mx
        rank = jnp.where(hit, float(r), rank)
        work = jnp.where(hit, -jnp.inf, work)
    return rank


def _gelu(x):
    return 0.5 * x * (1.0 + lax.erf(x * (2.0 ** -0.5)))


PEER_CAND_ROWS = 5 * PEER_TOPK
BF16_ROWS = 16
PEER_ROW_BLOCK = 4
PEER_GROUP_BLOCK = 4
PEER_LANES = LANES


def _peer_stats(qp_ref, keys_ref, m_ref, rank_ref, c0_ref, e1_ref, top_ref, cand_ref, best_ref):
    nk, kk, tl = PEER_N_KEYS, PEER_TOPK, PEER_LANES
    row8 = lax.broadcasted_iota(jnp.int32, (8, tl), 0)

    def chunk(c, carry):
        t0 = pl.multiple_of(c * tl, tl)
        for p in range(PEER_HEADS):
            scores = []
            for half in range(2):
                j = 2 * p + half
                qs = qp_ref[pl.ds(t0, tl), j * nk:(j + 1) * nk]
                scores.append(lax.dot_general(keys_ref[j], qs, NT_DIMS,
                                              preferred_element_type=F32))
            s0, s1 = scores
            _top_values(s0, kk, top_ref.at[0])
            rank_ref[c, p] = _top_values(s1, kk, top_ref.at[1]).astype(BF16)
            cand_ref[0:kk, :] = top_ref[0, 0:1, :] + top_ref[1]
            cand_ref[kk:kk + 8, :] = top_ref[0, 1:2, :] + top_ref[1, 0:8, :]
            for a in range(2, 8):
                grp = top_ref[0, a:a + 1, :] + top_ref[1, 0:8, :]
                cand_ref[8 + 8 * a:16 + 8 * a, :] = jnp.where(row8 < kk // (a + 1), grp, -jnp.inf)
            cand_ref[PEER_CAND_ROWS - 8:, :] = top_ref[0, 8:kk, :] + top_ref[1, 0:1, :]
            _top_values(cand_ref[...], kk, best_ref)
            best = best_ref[...]
            z = jnp.sum(jnp.exp(best - best[0:1, :]), axis=0, keepdims=True)
            thr = best[kk - 1:kk, :]
            m = jnp.zeros(s0.shape, F32)
            for b in range(4):
                m = m + jnp.where(s0 + top_ref[1, b:b + 1, :] >= thr, 1.0, 0.0)
            for a in range(3):
                tail = top_ref[0, a:a + 1, :] + top_ref[1, 4:kk // (a + 1), :]
                extra = jnp.sum(jnp.where(tail >= thr, 1.0, 0.0), axis=0, keepdims=True)
                m = m + jnp.where(s0 == top_ref[0, a:a + 1, :], extra, 0.0)
            m_ref[c, p] = m
            c0_ref[c, p] = jnp.exp(s0 - top_ref[0, 0:1, :]) / z
            e1_ref[c, p] = jnp.exp(s1 - top_ref[1, 0:1, :]).astype(BF16)
        return carry

    lax.fori_loop(0, m_ref.shape[0], chunk, 0)


def _peer_kernel(qp_ref, keys_ref, h2_ref, u_ref, vt_ref, x1_ref, gate_ref, fin_ref, o_ref,
                 m_ref, rank_ref, c0_ref, e1_ref, acc_ref, top_ref, cand_ref, best_ref, act_ref):
    eb = pl.program_id(1)
    n_eb = pl.num_programs(1)
    nk, tl = PEER_N_KEYS, PEER_LANES
    rows = u_ref.shape[0] // nk

    @pl.when(eb == 0)
    def _():
        acc_ref[...] = jnp.zeros_like(acc_ref)
        _peer_stats(qp_ref, keys_ref, m_ref, rank_ref, c0_ref, e1_ref, top_ref, cand_ref, best_ref)

    a_t = lax.dot_general(u_ref[...], h2_ref[...], NT_DIMS, preferred_element_type=F32)
    rb_n, jb_n, grp = PEER_ROW_BLOCK, PEER_GROUP_BLOCK, BF16_ROWS
    for c in range(m_ref.shape[0]):
        lanes = slice(c * tl, (c + 1) * tl)
        for rb in range(rows // rb_n):
            for jb in range(nk // (grp * jb_n)):
                acc = [[jnp.zeros((grp, tl), BF16) for _ in range(jb_n)] for _ in range(rb_n)]
                for p in range(PEER_HEADS):
                    c0rows, mrows = [], []
                    for k in range(rb_n):
                        i = eb * rows + rb * rb_n + k
                        c0rows.append(jnp.broadcast_to(
                            (0.5 * c0_ref[c, p, pl.ds(i, 1), :]).astype(BF16), (grp, tl)))
                        mrows.append(jnp.broadcast_to(
                            (2.0 * m_ref[c, p, pl.ds(i, 1), :]).astype(BF16), (grp, tl)))
                    for jj in range(jb_n):
                        j0 = (jb * jb_n + jj) * grp
                        rk = rank_ref[c, p, j0:j0 + grp, :]
                        rk = rk + rk
                        e1 = e1_ref[c, p, j0:j0 + grp, :]
                        e1 = e1 + e1
                        for k in range(rb_n):
                            val = c0rows[k] * e1
                            acc[k][jj] = acc[k][jj] + jnp.where(rk < mrows[k], val,
                                                                jnp.zeros_like(val))
                for k in range(rb_n):
                    for jj in range(jb_n):
                        r0 = (rb * rb_n + k) * nk + (jb * jb_n + jj) * grp
                        act_ref[r0:r0 + grp, lanes] = (
                            _gelu(a_t[r0:r0 + grp, lanes]).astype(BF16) * acc[k][jj])
    acc_ref[...] += jnp.dot(vt_ref[0], act_ref[...], preferred_element_type=F32)

    @pl.when(eb == n_eb - 1)
    def _():
        x2 = x1_ref[...] + gate_ref[0] * acc_ref[...].T
        ms = jnp.mean(x2 * x2, axis=-1, keepdims=True)
        o_ref[...] = x2 * lax.rsqrt(ms + NORM_EPS) * fin_ref[...]


def _peer(qp, keys, h2, u, vt, x1, gate2, fin, seq, tm):
    t, d = x1.shape
    te = vt.shape[2]
    per_b = seq // tm
    nk, tl = PEER_N_KEYS, PEER_LANES
    stat = pltpu.VMEM((tm // tl, PEER_HEADS, nk, tl), F32)
    stat16 = pltpu.VMEM((tm // tl, PEER_HEADS, nk, tl), BF16)
    once = pl.Buffered(1)
    return pl.pallas_call(
        _peer_kernel,
        out_shape=jax.ShapeDtypeStruct((t, d), F32),
        grid=(t // tm, u.shape[0] // te),
        in_specs=[pl.BlockSpec((tm, qp.shape[1]), lambda i, e: (i, 0), pipeline_mode=once),
                  pl.BlockSpec(keys.shape, lambda i, e: (0, 0, 0), pipeline_mode=once),
                  pl.BlockSpec((tm, d), lambda i, e: (i, 0)),
                  pl.BlockSpec((te, d), lambda i, e: (e, 0)),
                  pl.BlockSpec((1, d, te), lambda i, e: (e, 0, 0)),
                  pl.BlockSpec((tm, d), lambda i, e: (i, 0), pipeline_mode=once),
                  pl.BlockSpec((1, 1, d), lambda i, e: (i // per_b, 0, 0)),
                  pl.BlockSpec((1, d), lambda i, e: (0, 0))],
        out_specs=pl.BlockSpec((tm, d), lambda i, e: (i, 0)),
        scratch_shapes=[stat, stat16, stat, stat16,
                        pltpu.VMEM((d, tm), F32),
                        pltpu.VMEM((2, PEER_TOPK, tl), F32),
                        pltpu.VMEM((PEER_CAND_ROWS, tl), F32),
                        pltpu.VMEM((PEER_TOPK, tl), F32),
                        pltpu.VMEM((te, tm), BF16)],
        compiler_params=_params(("arbitrary", "arbitrary")),
        name="peer",
    )(qp, keys, h2, u, vt, x1, gate2, fin.reshape(1, d))


def kernel(x, c, w_ada, b_ada, norm_mix, norm_ffn, w_in, lb_logits, hgrn_gain, w_up_a, w_up_b,
           w_out, peer_w_q, peer_keys, peer_u, peer_v, final_norm):
    bsz, seq, d = x.shape
    depth = w_ada.shape[0]
    assert depth == 1, "the PEER kernel applies the final norm, so it must be the last layer"
    aw = A_HEADS * HEAD_DIM
    bw = B_HEADS * HEAD_DIM
    iw = IDX_HEADS * IDX_DIM
    lower_bounds = jnp.cumsum(jax.nn.softmax(lb_logits.astype(F32), axis=0), axis=0)
    xt = x.reshape(bsz * seq, d)

    widths = (aw, aw, aw, iw, IDX_DIM, IDX_HEADS, bw, bw, bw, bw, d, d)
    names = ("qa", "ka", "va", "qi", "ki", "wi", "qb", "fb", "ib", "gb", "g_a", "g_b")
    off, start = {}, 0
    for nme, wd in zip(names, widths):
        off[nme] = (start, start + wd)
        start += wd
    order1 = ("g_a", "g_b", "qa", "ka", "va", "qi", "qb", "ib", "gb")
    order2 = ("fb", "ki", "wi")
    pad2 = LANES - IDX_DIM - IDX_HEADS

    def block_ids(order, block):
        ids, pos = {}, 0
        for nme in order:
            if pos % block == 0:
                ids[nme] = pos // block
            pos += off[nme][1] - off[nme][0]
        return ids

    for l in range(depth):
        mod = _ada(c, w_ada[l], b_ada[l], tn=TILES["ada_tn"])
        shift1, scale1, gate1, shift2, scale2, gate2 = [
            m.reshape(bsz, 1, d) for m in jnp.split(mod, 6, axis=-1)]

        wl = w_in[l].astype(BF16)
        w1 = jnp.concatenate([wl[:, off[n][0]:off[n][1]] for n in order1], axis=1)
        w2 = jnp.concatenate([wl[:, off[n][0]:off[n][1]] for n in order2]
                             + [jnp.zeros((d, pad2), BF16)], axis=1)
        p1, p2 = _inproj(xt, norm_mix[l], scale1, shift1, w1, w2, seq,
                         tm=TILES["inproj_tm"], tn=TILES["inproj_tn"])

        wide, head, lane2 = block_ids(order1, aw), block_ids(order1, HEAD_DIM), block_ids(order2, LANES)
        attn = _dsa(p1, p2, bsz, seq, {"qa": wide["qa"], "ka": wide["ka"], "va": wide["va"],
                                       "qi": wide["qi"], "kw": lane2["ki"]})
        rec = _hgrn(p1, p2, lower_bounds[l], hgrn_gain[l], bsz, seq,
                    {"qb": head["qb"], "ib": head["ib"], "gb": head["gb"], "fb": lane2["fb"]})

        x1, h2, qp = _mix(attn, rec, p1, xt, gate1,
                          w_up_a[l].astype(BF16), w_up_b[l].astype(BF16), w_out[l].astype(BF16),
                          norm_ffn[l], scale2, shift2, peer_w_q[l].astype(BF16),
                          seq, tm=TILES["mix_tm"], col=block_ids(order1, d))
        keys = peer_keys[l].reshape(2 * PEER_HEADS, PEER_N_KEYS, -1).astype(BF16)
        vt = peer_v[l].astype(BF16).reshape(-1, TILES["peer_te"], d).transpose(0, 2, 1)
        xt = _peer(qp, keys, h2, peer_u[l].astype(BF16), vt, x1, gate2, final_norm, seq,
                   tm=TILES["peer_tm"])
    return xt.reshape(bsz, seq, d)
```

```python
import jax
import jax.numpy as jnp
from jax import lax
from jax.experimental import pallas as pl
from jax.experimental.pallas import tpu as pltpu

F32 = jnp.float32
BF16 = jnp.bfloat16

NORM_EPS = 1e-6
CHUNK = 64
TOPK_MAX = 256
A_HEADS = 8
HEAD_DIM = 128
IDX_HEADS = 16
IDX_DIM = 64
B_HEADS = 8
PEER_HEADS = 8
PEER_N_KEYS = 128
PEER_TOPK = 16
SUB = 16

LANES = 128
SUBLANES = 8
VMEM_LIMIT = 56 * 1024 * 1024
INT_MIN = -2 ** 31

TILES = dict(
    ada_tn=1024,
    inproj_tm=1024, inproj_tn=1024,
    mix_tm=256,
    peer_tm=512,
    peer_te=1024,
)

NT_DIMS = (((1,), (1,)), ((), ()))
TN_DIMS = (((0,), (0,)), ((), ()))


def _sigmoid(x):
    return 1.0 / (1.0 + jnp.exp(-x))


def _params(sem):
    return pltpu.CompilerParams(dimension_semantics=sem, vmem_limit_bytes=VMEM_LIMIT)


def _ada_kernel(c_ref, w_ref, b_ref, o_ref):
    c = c_ref[...]
    cond = (c * _sigmoid(c)).astype(BF16)
    o_ref[...] = jnp.dot(cond, w_ref[...].astype(BF16), preferred_element_type=F32) + b_ref[...]


def _ada(c, w, b, tn):
    bsz, d = c.shape
    n = w.shape[1]
    return pl.pallas_call(
        _ada_kernel,
        out_shape=jax.ShapeDtypeStruct((bsz, n), F32),
        grid=(n // tn,),
        in_specs=[pl.BlockSpec((bsz, d), lambda j: (0, 0)),
                  pl.BlockSpec((d, tn), lambda j: (0, j)),
                  pl.BlockSpec((1, tn), lambda j: (0, j))],
        out_specs=pl.BlockSpec((bsz, tn), lambda j: (0, j)),
        compiler_params=_params(("arbitrary",)),
        name="ada",
    )(c, w, b.reshape(1, n))


def _norm_mod(x, gain, scale, shift):
    ms = jnp.mean(x * x, axis=-1, keepdims=True)
    y = x * lax.rsqrt(ms + NORM_EPS) * gain
    return y * (1.0 + scale) + shift


def _inproj_kernel(x_ref, g_ref, sc_ref, sh_ref, w1_ref, w2_ref, o1_ref, o2_ref, h_ref):
    j = pl.program_id(1)
    n1 = pl.num_programs(1) - 1

    @pl.when(j == 0)
    def _():
        h_ref[...] = _norm_mod(x_ref[...], g_ref[...], sc_ref[0], sh_ref[0]).astype(BF16)

    @pl.when(j < n1)
    def _():
        o1_ref[...] = jnp.dot(h_ref[...], w1_ref[...],
                              preferred_element_type=F32).astype(o1_ref.dtype)

    @pl.when(j == n1)
    def _():
        o2_ref[...] = jnp.dot(h_ref[...], w2_ref[...], preferred_element_type=F32)


def _inproj(x2d, gain, scale, shift, w1, w2, seq, tm, tn):
    t, d = x2d.shape
    n1 = w1.shape[1] // tn
    n2 = w2.shape[1]
    per_b = seq // tm
    return pl.pallas_call(
        _inproj_kernel,
        out_shape=(jax.ShapeDtypeStruct((t, w1.shape[1]), BF16),
                   jax.ShapeDtypeStruct((t, n2), F32)),
        grid=(t // tm, n1 + 1),
        in_specs=[pl.BlockSpec((tm, d), lambda i, j: (i, 0)),
                  pl.BlockSpec((1, d), lambda i, j: (0, 0)),
                  pl.BlockSpec((1, 1, d), lambda i, j: (i // per_b, 0, 0)),
                  pl.BlockSpec((1, 1, d), lambda i, j: (i // per_b, 0, 0)),
                  pl.BlockSpec((d, tn), lambda i, j: (0, jnp.minimum(j, n1 - 1))),
                  pl.BlockSpec((d, n2), lambda i, j: (0, 0))],
        out_specs=(pl.BlockSpec((tm, tn), lambda i, j: (i, jnp.minimum(j, n1 - 1))),
                   pl.BlockSpec((tm, n2), lambda i, j: (i, 0))),
        scratch_shapes=[pltpu.VMEM((tm, d), BF16)],
        compiler_params=_params(("arbitrary", "arbitrary")),
        name="inproj",
    )(x2d, gain.reshape(1, d), scale, shift, w1, w2)


DSA_QB = 256
DSA_KT = 256
DSA_AT = LANES
NEG_BIG = -1e30


def _dsa_kernel(qi_ref, kw_ref, qa_ref, ka_ref, va_ref, o_ref,
                score_ref, vt_ref, m_ref, l_ref, acc_ref):
    j = pl.program_id(1)
    qb, kt_rows = DSA_QB, DSA_KT
    seq = ka_ref.shape[0]
    n_kt = (j * qb + qb + kt_rows - 1) // kt_rows
    r0 = pl.multiple_of(j * qb, qb)

    @pl.when(j == 0)
    def _():
        for h in range(A_HEADS):
            for t in range(seq // kt_rows):
                blk = va_ref[t * kt_rows:(t + 1) * kt_rows, h * HEAD_DIM:(h + 1) * HEAD_DIM]
                vt_ref[h * HEAD_DIM:(h + 1) * HEAD_DIM, t * kt_rows:(t + 1) * kt_rows] = (
                    blk.astype(F32).T.astype(BF16))

    w_t = kw_ref[pl.ds(r0, qb), :].T
    q_chunk = (r0 + lax.broadcasted_iota(jnp.int32, (kt_rows, qb), 1)) // CHUNK
    key_iota = lax.broadcasted_iota(jnp.int32, (kt_rows, qb), 0)

    def score_tile(kt, carry):
        k0 = pl.multiple_of(kt * kt_rows, kt_rows)
        ki = kw_ref[pl.ds(k0, kt_rows), 0:IDX_DIM].astype(BF16)
        sc = jnp.zeros((kt_rows, qb), F32)
        for h in range(IDX_HEADS):
            dots = lax.dot_general(ki, qi_ref[:, h * IDX_DIM:(h + 1) * IDX_DIM], NT_DIMS,
                                   preferred_element_type=F32)
            sc = sc + w_t[IDX_DIM + h:IDX_DIM + h + 1, :] * jnp.maximum(dots, 0.0)
        sc = sc * ((IDX_DIM * IDX_HEADS) ** -0.5)
        adm = (k0 + key_iota) // CHUNK <= q_chunk
        score_ref[pl.ds(k0, kt_rows), :] = jnp.where(adm, sc, -jnp.inf)
        return carry

    lax.fori_loop(0, n_kt, score_tile, 0)

    def key_to_float(key):
        return pltpu.bitcast(key ^ ((key >> 31) & jnp.int32(0x7FFFFFFF)), F32)

    top_k = float(min(TOPK_MAX, seq // 4))

    def bit_step(it, v):
        cand = v + lax.shift_left(jnp.int32(1), 31 - it)
        cand_f = key_to_float(cand)

        def count_tile(kt, part):
            k0 = pl.multiple_of(kt * kt_rows, kt_rows)
            hit = jnp.where(score_ref[pl.ds(k0, kt_rows), :] >= cand_f, 1.0, 0.0)
            return part + jnp.sum(hit.reshape(kt_rows // SUBLANES, SUBLANES, qb), axis=0)

        part = lax.fori_loop(0, n_kt, count_tile, jnp.zeros((SUBLANES, qb), F32))
        cnt = jnp.sum(part, axis=0, keepdims=True)
        return jnp.where(cnt >= top_k, cand, v)

    v = lax.fori_loop(0, 32, bit_step, jnp.full((1, qb), INT_MIN, jnp.int32))
    thr = jnp.where(v == INT_MIN, jnp.finfo(F32).min, key_to_float(v))

    att_scale = HEAD_DIM ** -0.5
    m_ref[...] = jnp.full(m_ref.shape, NEG_BIG, F32)
    l_ref[...] = jnp.zeros(l_ref.shape, F32)
    acc_ref[...] = jnp.zeros(acc_ref.shape, F32)

    at_rows = DSA_AT

    def att_tile(kt, carry):
        k0 = pl.multiple_of(kt * at_rows, at_rows)
        bias = jnp.where(score_ref[pl.ds(k0, at_rows), :] >= thr, 0.0, NEG_BIG)
        for h in range(A_HEADS):
            cols = slice(h * HEAD_DIM, (h + 1) * HEAD_DIM)
            st = lax.dot_general(ka_ref[pl.ds(k0, at_rows), cols], qa_ref[:, cols], NT_DIMS,
                                 preferred_element_type=F32) * att_scale + bias
            m_old = m_ref[h:h + 1, :]
            m_new = jnp.maximum(m_old, jnp.max(st, axis=0, keepdims=True))
            alpha = jnp.exp(m_old - m_new)
            p = jnp.exp(st - m_new)
            m_ref[h:h + 1, :] = m_new
            l_ref[h:h + 1, :] = l_ref[h:h + 1, :] * alpha + jnp.sum(p, axis=0, keepdims=True)
            acc_ref[cols, :] = acc_ref[cols, :] * alpha + jnp.dot(
                vt_ref[cols, pl.ds(k0, at_rows)], p.astype(BF16), preferred_element_type=F32)
        return carry

    lax.fori_loop(0, n_kt * (kt_rows // at_rows), att_tile, 0)
    for h in range(A_HEADS):
        cols = slice(h * HEAD_DIM, (h + 1) * HEAD_DIM)
        o_ref[:, cols] = (acc_ref[cols, :] / l_ref[h:h + 1, :]).T.astype(o_ref.dtype)


def _dsa(p1, p2, bsz, seq, col):
    n_blk = seq // DSA_QB
    width = A_HEADS * HEAD_DIM
    return pl.pallas_call(
        _dsa_kernel,
        out_shape=jax.ShapeDtypeStruct((bsz * seq, width), BF16),
        grid=(bsz, n_blk),
        in_specs=[pl.BlockSpec((DSA_QB, width), lambda b, k: (b * n_blk + k, col["qi"])),
                  pl.BlockSpec((seq, LANES), lambda b, k: (b, col["kw"])),
                  pl.BlockSpec((DSA_QB, width), lambda b, k: (b * n_blk + k, col["qa"])),
                  pl.BlockSpec((seq, width), lambda b, k: (b, col["ka"])),
                  pl.BlockSpec((seq, width), lambda b, k: (b, col["va"]))],
        out_specs=pl.BlockSpec((DSA_QB, width), lambda b, k: (b * n_blk + k, 0)),
        scratch_shapes=[pltpu.VMEM((seq, DSA_QB), F32), pltpu.VMEM((width, seq), BF16),
                        pltpu.VMEM((A_HEADS, DSA_QB), F32), pltpu.VMEM((A_HEADS, DSA_QB), F32),
                        pltpu.VMEM((width, DSA_QB), F32)],
        compiler_params=_params(("arbitrary", "arbitrary")),
        name="dsa",
    )(p1, p2, p1, p1, p1)


HGRN_GROUP = 4


def _hgrn_kernel(q_ref, f_ref, i_ref, g_ref, lb_ref, gain_ref, o_ref,
                 qe_ref, oi_ref, upd_ref, dec_ref):
    c = CHUNK
    n_c = q_ref.shape[0] // c
    d = q_ref.shape[1]
    lb = lb_ref[0]
    gain = gain_ref[...]
    chunk_row = lax.broadcasted_iota(jnp.int32, (c, d), 0)
    sub_row = lax.broadcasted_iota(jnp.int32, (SUB, d), 0)

    def local_part(ci):
        r0 = pl.multiple_of(ci * c, c)
        f = lb + (1.0 - lb) * _sigmoid(f_ref[pl.ds(r0, c), :])
        logf = jnp.log(f)
        k = 1.0 - f
        qr = q_ref[pl.ds(r0, c), :].astype(F32)
        q = qr * _sigmoid(qr)
        v = i_ref[pl.ds(r0, c), :].astype(F32)
        v16 = v.astype(BF16)
        b = logf
        shift = 1
        while shift < c:
            b = b + jnp.where(chunk_row >= shift, pltpu.roll(b, shift, 0), 0.0)
            shift *= 2
        b_last = b[c - 1:c, :]
        qe_ref[pl.ds(r0, c), :] = (q * jnp.exp(b)).astype(BF16)

        for blk in range(c // SUB):
            lo = blk * SUB
            q_i, k_i, v_i, b_i = q[lo:lo + SUB], k[lo:lo + SUB], v[lo:lo + SUB], b[lo:lo + SUB]
            o_i = jnp.zeros((SUB, d), F32)
            if blk > 0:
                b_ref = b[lo - 1:lo, :]
                qt = (q_i * jnp.exp(b_i - b_ref)).astype(BF16)
                kt = (k[:lo] * jnp.exp(b_ref - b[:lo])).astype(BF16)
                att = lax.dot_general(qt, kt, NT_DIMS, preferred_element_type=F32)
                o_i = jnp.dot(att.astype(BF16), v16[:lo], preferred_element_type=F32)
            for s_ in range(SUB):
                w = jnp.exp(jnp.where(sub_row >= s_, b_i - b_i[s_:s_ + 1, :], -jnp.inf))
                a = jnp.sum(q_i * k_i[s_:s_ + 1, :] * w, axis=1, keepdims=True)
                o_i = o_i + a * v_i[s_:s_ + 1, :]
            oi_ref[pl.ds(r0 + lo, SUB), :] = o_i

        k_end = (k * jnp.exp(b_last - b)).astype(BF16)
        upd_ref[ci] = lax.dot_general(v16, k_end, TN_DIMS, preferred_element_type=F32)
        dec_ref[ci] = jnp.exp(b_last)

    def group(gi, carry):
        for u in range(HGRN_GROUP):
            local_part(gi * HGRN_GROUP + u)
        return carry

    lax.fori_loop(0, n_c // HGRN_GROUP, group, 0)

    st_t = jnp.zeros((d, d), F32)
    for ci in range(n_c):
        rows = slice(ci * c, (ci + 1) * c)
        o = oi_ref[rows, :] + lax.dot_general(qe_ref[rows, :], st_t.astype(BF16), NT_DIMS,
                                              preferred_element_type=F32)
        ms = jnp.mean(o * o, axis=-1, keepdims=True)
        rec = o * lax.rsqrt(ms + NORM_EPS) * gain
        gr = g_ref[rows, :].astype(F32)
        o_ref[rows, :] = (rec * (gr * _sigmoid(gr))).astype(o_ref.dtype)
        st_t = st_t * dec_ref[ci] + upd_ref[ci]


def _hgrn(p1, p2, lb, gain, bsz, seq, col):
    d = HEAD_DIM
    n_c = seq // CHUNK
    return pl.pallas_call(
        _hgrn_kernel,
        out_shape=jax.ShapeDtypeStruct((bsz * seq, B_HEADS * d), BF16),
        grid=(bsz, B_HEADS),
        in_specs=[pl.BlockSpec((seq, d), lambda b, h: (b, col["qb"] + h)),
                  pl.BlockSpec((seq, d), lambda b, h: (b, col["fb"] + h)),
                  pl.BlockSpec((seq, d), lambda b, h: (b, col["ib"] + h)),
                  pl.BlockSpec((seq, d), lambda b, h: (b, col["gb"] + h)),
                  pl.BlockSpec((1, 1, d), lambda b, h: (h, 0, 0)),
                  pl.BlockSpec((1, d), lambda b, h: (0, 0))],
        out_specs=pl.BlockSpec((seq, d), lambda b, h: (b, h)),
        scratch_shapes=[pltpu.VMEM((seq, d), BF16), pltpu.VMEM((seq, d), F32),
                        pltpu.VMEM((n_c, d, d), F32), pltpu.VMEM((n_c, 1, d), F32)],
        compiler_params=_params(("arbitrary", "arbitrary")),
        name="hgrn",
    )(p1, p2, p1, p1, lb.reshape(B_HEADS, 1, d), gain.reshape(1, d))


def _mix_kernel(attn_ref, rec_ref, ga_ref, gb_ref, x_ref, gate_ref, wa_ref, wb_ref, wo_ref,
                g2_ref, sc_ref, sh_ref, wq_ref, x1_ref, h2_ref, qp_ref):
    ya = jnp.dot(attn_ref[...], wa_ref[...], preferred_element_type=F32)
    yb = jnp.dot(rec_ref[...], wb_ref[...], preferred_element_type=F32)
    mixed = _sigmoid(ga_ref[...].astype(F32)) * ya + _sigmoid(gb_ref[...].astype(F32)) * yb
    x1 = x_ref[...] + gate_ref[0] * jnp.dot(mixed.astype(BF16), wo_ref[...],
                                            preferred_element_type=F32)
    x1_ref[...] = x1
    h2 = _norm_mod(x1, g2_ref[...], sc_ref[0], sh_ref[0]).astype(BF16)
    h2_ref[...] = h2
    qp_ref[...] = jnp.dot(h2, wq_ref[...], preferred_element_type=F32).astype(qp_ref.dtype)


def _mix(attn, rec, p1, x2d, gate1, wa, wb, wo, g2, scale2, shift2, wq, seq, tm, col):
    t, d = x2d.shape
    wa_w = attn.shape[1]
    per_b = seq // tm
    full = lambda shape: pl.BlockSpec(shape, lambda i: (0,) * len(shape),
                                      pipeline_mode=pl.Buffered(1))
    mod = pl.BlockSpec((1, 1, d), lambda i: (i // per_b, 0, 0))
    return pl.pallas_call(
        _mix_kernel,
        out_shape=(jax.ShapeDtypeStruct((t, d), F32), jax.ShapeDtypeStruct((t, d), BF16),
                   jax.ShapeDtypeStruct((t, wq.shape[1]), BF16)),
        grid=(t // tm,),
        in_specs=[pl.BlockSpec((tm, wa_w), lambda i: (i, 0)),
                  pl.BlockSpec((tm, wa_w), lambda i: (i, 0)),
                  pl.BlockSpec((tm, d), lambda i: (i, col["g_a"])),
                  pl.BlockSpec((tm, d), lambda i: (i, col["g_b"])),
                  pl.BlockSpec((tm, d), lambda i: (i, 0)),
                  mod, full(wa.shape), full(wb.shape), full(wo.shape),
                  full((1, d)), mod, mod, full(wq.shape)],
        out_specs=(pl.BlockSpec((tm, d), lambda i: (i, 0)),
                   pl.BlockSpec((tm, d), lambda i: (i, 0)),
                   pl.BlockSpec((tm, wq.shape[1]), lambda i: (i, 0))),
        compiler_params=_params(("arbitrary",)),
        name="mix",
    )(attn, rec, p1, p1, x2d, gate1, wa, wb, wo, g2.reshape(1, d), scale2, shift2, wq)


def _top_values(work, n, dst_ref):
    rank = jnp.full(work.shape, float(n), F32)
    for r in range(n):
        mx = jnp.max(work, axis=0, keepdims=True)
        dst_ref[r:r + 1, :] = mx
        hit = work == mx
        rank = jnp.where(hit, float(r), rank)
        work = jnp.where(hit, -jnp.inf, work)
    return rank


def _gelu(x):
    return 0.5 * x * (1.0 + lax.erf(x * (2.0 ** -0.5)))


PEER_CAND_ROWS = PEER_TOPK + (PEER_TOPK // 2) ** 2
PEER_LANES = LANES


def _peer_stats(qp_ref, keys_ref, m_ref, rank_ref, c0_ref, e1_ref, top_ref, cand_ref, best_ref):
    nk, kk, tl = PEER_N_KEYS, PEER_TOPK, PEER_LANES
    half = kk // 2
    row_half = lax.broadcasted_iota(jnp.int32, (half, tl), 0)

    def chunk(c, carry):
        t0 = pl.multiple_of(c * tl, tl)
        for p in range(PEER_HEADS):
            scores = []
            for side in range(2):
                j = 2 * p + side
                qs = qp_ref[pl.ds(t0, tl), j * nk:(j + 1) * nk]
                scores.append(lax.dot_general(keys_ref[j], qs, NT_DIMS,
                                              preferred_element_type=F32))
            s0, s1 = scores
            _top_values(s0, kk, top_ref.at[0])
            rank_ref[c, p] = _top_values(s1, kk, top_ref.at[1]).astype(BF16)
            cand_ref[0:kk, :] = top_ref[0, 0:1, :] + top_ref[1]
            cand_ref[kk:kk + half, :] = top_ref[0, 1:2, :] + top_ref[1, 0:half, :]
            for a in range(2, half):
                grp = top_ref[0, a:a + 1, :] + top_ref[1, 0:half, :]
                cand_ref[half * (a + 1):half * (a + 2), :] = jnp.where(
                    row_half < kk // (a + 1), grp, -jnp.inf)
            cand_ref[PEER_CAND_ROWS - half:, :] = top_ref[0, half:kk, :] + top_ref[1, 0:1, :]
            _top_values(cand_ref[...], kk, best_ref)
            best = best_ref[...]
            z = jnp.sum(jnp.exp(best - best[0:1, :]), axis=0, keepdims=True)
            thr = best[kk - 1:kk, :]
            m = jnp.zeros(s0.shape, F32)
            for b in range(4):
                m = m + jnp.where(s0 + top_ref[1, b:b + 1, :] >= thr, 1.0, 0.0)
            for a in range(3):
                tail = top_ref[0, a:a + 1, :] + top_ref[1, 4:kk // (a + 1), :]
                extra = jnp.sum(jnp.where(tail >= thr, 1.0, 0.0), axis=0, keepdims=True)
                m = m + jnp.where(s0 == top_ref[0, a:a + 1, :], extra, 0.0)
            m_ref[c, p] = m
            c0_ref[c, p] = jnp.exp(s0 - top_ref[0, 0:1, :]) / z
            e1_ref[c, p] = jnp.exp(s1 - top_ref[1, 0:1, :]).astype(BF16)
        return carry

    lax.fori_loop(0, m_ref.shape[0], chunk, 0)


def _peer_kernel(qp_ref, keys_ref, h2_ref, u_ref, vt_ref, x1_ref, gate_ref, fin_ref, o_ref,
                 m_ref, rank_ref, c0_ref, e1_ref, acc_ref, top_ref, cand_ref, best_ref, act_ref):
    eb = pl.program_id(1)
    n_eb = pl.num_programs(1)
    nk, tl = PEER_N_KEYS, PEER_LANES
    rows = u_ref.shape[0] // nk

    @pl.when(eb == 0)
    def _():
        acc_ref[...] = jnp.zeros_like(acc_ref)
        _peer_stats(qp_ref, keys_ref, m_ref, rank_ref, c0_ref, e1_ref, top_ref, cand_ref, best_ref)

    a_t = lax.dot_general(u_ref[...], h2_ref[...], NT_DIMS, preferred_element_type=F32)
    for r in range(rows):
        i = eb * rows + r
        sl = slice(r * nk, (r + 1) * nk)
        for c in range(m_ref.shape[0]):
            lanes = slice(c * tl, (c + 1) * tl)
            gmat = jnp.zeros((nk, tl), BF16)
            for p in range(PEER_HEADS):
                c0row = jnp.broadcast_to(c0_ref[c, p, pl.ds(i, 1), :].astype(BF16), (nk, tl))
                mrow = jnp.broadcast_to(m_ref[c, p, pl.ds(i, 1), :].astype(BF16), (nk, tl))
                val = c0row * e1_ref[c, p]
                gmat = gmat + jnp.where(rank_ref[c, p] < mrow, val, jnp.zeros_like(val))
            act_ref[sl, lanes] = _gelu(a_t[sl, lanes]).astype(BF16) * gmat
    acc_ref[...] += jnp.dot(vt_ref[0], act_ref[...], preferred_element_type=F32)

    @pl.when(eb == n_eb - 1)
    def _():
        x2 = x1_ref[...] + gate_ref[0] * acc_ref[...].T
        ms = jnp.mean(x2 * x2, axis=-1, keepdims=True)
        o_ref[...] = x2 * lax.rsqrt(ms + NORM_EPS) * fin_ref[...]


def _peer(qp, keys, h2, u, vt, x1, gate2, fin, seq, tm):
    t, d = x1.shape
    te = vt.shape[2]
    per_b = seq // tm
    nk, tl = PEER_N_KEYS, PEER_LANES
    stat = pltpu.VMEM((tm // tl, PEER_HEADS, nk, tl), F32)
    stat16 = pltpu.VMEM((tm // tl, PEER_HEADS, nk, tl), BF16)
    once = pl.Buffered(1)
    return pl.pallas_call(
        _peer_kernel,
        out_shape=jax.ShapeDtypeStruct((t, d), F32),
        grid=(t // tm, u.shape[0] // te),
        in_specs=[pl.BlockSpec((tm, qp.shape[1]), lambda i, e: (i, 0), pipeline_mode=once),
                  pl.BlockSpec(keys.shape, lambda i, e: (0, 0, 0), pipeline_mode=once),
                  pl.BlockSpec((tm, d), lambda i, e: (i, 0)),
                  pl.BlockSpec((te, d), lambda i, e: (e, 0)),
                  pl.BlockSpec((1, d, te), lambda i, e: (e, 0, 0)),
                  pl.BlockSpec((tm, d), lambda i, e: (i, 0), pipeline_mode=once),
                  pl.BlockSpec((1, 1, d), lambda i, e: (i // per_b, 0, 0)),
                  pl.BlockSpec((1, d), lambda i, e: (0, 0))],
        out_specs=pl.BlockSpec((tm, d), lambda i, e: (i, 0)),
        scratch_shapes=[stat, stat16, stat, stat16,
                        pltpu.VMEM((d, tm), F32),
                        pltpu.VMEM((2, PEER_TOPK, tl), F32),
                        pltpu.VMEM((PEER_CAND_ROWS, tl), F32),
                        pltpu.VMEM((PEER_TOPK, tl), F32),
                        pltpu.VMEM((te, tm), BF16)],
        compiler_params=_params(("arbitrary", "arbitrary")),
        name="peer",
    )(qp, keys, h2, u, vt, x1, gate2, fin.reshape(1, d))


def kernel(x, c, w_ada, b_ada, norm_mix, norm_ffn, w_in, lb_logits, hgrn_gain, w_up_a, w_up_b,
           w_out, peer_w_q, peer_keys, peer_u, peer_v, final_norm):
    bsz, seq, d = x.shape
    depth = w_ada.shape[0]
    assert depth == 1, "the PEER kernel applies the final norm, so it must be the last layer"
    aw = A_HEADS * HEAD_DIM
    bw = B_HEADS * HEAD_DIM
    iw = IDX_HEADS * IDX_DIM
    lower_bounds = jnp.cumsum(jax.nn.softmax(lb_logits.astype(F32), axis=0), axis=0)
    xt = x.reshape(bsz * seq, d)

    widths = (aw, aw, aw, iw, IDX_DIM, IDX_HEADS, bw, bw, bw, bw, d, d)
    names = ("qa", "ka", "va", "qi", "ki", "wi", "qb", "fb", "ib", "gb", "g_a", "g_b")
    off, start = {}, 0
    for nme, wd in zip(names, widths):
        off[nme] = (start, start + wd)
        start += wd
    order1 = ("g_a", "g_b", "qa", "ka", "va", "qi", "qb", "ib", "gb")
    order2 = ("fb", "ki", "wi")
    pad2 = LANES - IDX_DIM - IDX_HEADS

    def block_ids(order, block):
        ids, pos = {}, 0
        for nme in order:
            if pos % block == 0:
                ids[nme] = pos // block
            pos += off[nme][1] - off[nme][0]
        return ids

    for l in range(depth):
        mod = _ada(c, w_ada[l], b_ada[l], tn=TILES["ada_tn"])
        shift1, scale1, gate1, shift2, scale2, gate2 = [
            m.reshape(bsz, 1, d) for m in jnp.split(mod, 6, axis=-1)]

        wl = w_in[l].astype(BF16)
        w1 = jnp.concatenate([wl[:, off[n][0]:off[n][1]] for n in order1], axis=1)
        w2 = jnp.concatenate([wl[:, off[n][0]:off[n][1]] for n in order2]
                             + [jnp.zeros((d, pad2), BF16)], axis=1)
        p1, p2 = _inproj(xt, norm_mix[l], scale1, shift1, w1, w2, seq,
                         tm=TILES["inproj_tm"], tn=TILES["inproj_tn"])

        wide, head, lane2 = block_ids(order1, aw), block_ids(order1, HEAD_DIM), block_ids(order2, LANES)
        attn = _dsa(p1, p2, bsz, seq, {"qa": wide["qa"], "ka": wide["ka"], "va": wide["va"],
                                       "qi": wide["qi"], "kw": lane2["ki"]})
        rec = _hgrn(p1, p2, lower_bounds[l], hgrn_gain[l], bsz, seq,
                    {"qb": head["qb"], "ib": head["ib"], "gb": head["gb"], "fb": lane2["fb"]})

        x1, h2, qp = _mix(attn, rec, p1, xt, gate1,
                          w_up_a[l].astype(BF16), w_up_b[l].astype(BF16), w_out[l].astype(BF16),
                          norm_ffn[l], scale2, shift2, peer_w_q[l].astype(BF16),
                          seq, tm=TILES["mix_tm"], col=block_ids(order1, d))
        keys = peer_keys[l].reshape(2 * PEER_HEADS, PEER_N_KEYS, -1).astype(BF16)
        vt = peer_v[l].astype(BF16).reshape(-1, TILES["peer_te"], d).transpose(0, 2, 1)
        xt = _peer(qp, keys, h2, peer_u[l].astype(BF16), vt, x1, gate2, final_norm, seq,
                   tm=TILES["peer_tm"])
    return xt.reshape(bsz, seq, d)
```
